```python
import numpy as np
import jax, jax.numpy as jnp
from jax import lax

D_MODEL = 4096
BATCH = 32
SEQ = 256
DEPTH = 4
DEC_BATCH = 8
DEC_SEQ = 1024
PAST_LEN = 512

GRID_W = 64
N_MIXERS = 3
N_HEADS = 32
HEAD_DIM = D_MODEL // N_HEADS
WIN_H = 8
WIN_W = 16
COL_QBLK = 16
COL_KBLK = WIN_W + COL_QBLK
CTX_QBLK = 128
POOL_WINDOWS = (2, 4, 8, 16)
N_POOL_GROUPS = len(POOL_WINDOWS)
POOL_GROUP = D_MODEL // N_POOL_GROUPS
CONV_WIDTH = 31
FFN_DIM = 4 * D_MODEL
N_MOD = 6
EPS = 1e-6

kernel_name = "hybrid_natten_pool_conformer_dit_step"


def rmsnorm(x, g):
    xf = x.astype(jnp.float32)
    y = xf * lax.rsqrt(jnp.mean(xf * xf, axis=-1, keepdims=True) + EPS)
    return y.astype(x.dtype) * g


def adaln(cond, w_ada, b_ada):
    m = jax.nn.silu(cond) @ w_ada + b_ada
    m = m.reshape(cond.shape[0], N_MOD, 1, D_MODEL)
    return [m[:, j] for j in range(N_MOD)]


def modulate(h, shift, scale):
    return h * (1 + scale) + shift


def split_qkv(h, w_qkv):
    B, L, _ = h.shape
    qkv = (h @ w_qkv).reshape(B, L, 3, N_HEADS, HEAD_DIM)
    return qkv[:, :, 0], qkv[:, :, 1], qkv[:, :, 2]


def ctx_self_attention(q, k, v):
    B, L, H, Dh = q.shape
    nb = L // CTX_QBLK
    scale = Dh ** -0.5
    qb = jnp.moveaxis(q.reshape(B, nb, CTX_QBLK, H, Dh), 1, 0)

    def one(qblk):
        s = jnp.einsum('bqhd,bkhd->bhqk', qblk, k).astype(jnp.float32) * scale
        p = jax.nn.softmax(s, axis=-1).astype(v.dtype)
        return jnp.einsum('bhqk,bkhd->bqhd', p, v)

    o = lax.map(one, qb)
    return jnp.moveaxis(o, 0, 1).reshape(B, L, H, Dh)


def _col_tables():
    ncb = GRID_W // COL_QBLK
    cols = np.arange(GRID_W)
    start_c = np.clip(cols - WIN_W // 2, 0, GRID_W - WIN_W)
    qcol = np.arange(ncb)[:, None] * COL_QBLK + np.arange(COL_QBLK)[None, :]
    kstart = np.clip(np.arange(ncb) * COL_QBLK - WIN_W // 2, 0, GRID_W - COL_KBLK)
    kcol = kstart[:, None] + np.arange(COL_KBLK)[None, :]
    sc = start_c[qcol][:, :, None]
    kc = kcol[:, None, :]
    valid = (kc >= sc) & (kc < sc + WIN_W)
    dc_idx = np.clip(kc - qcol[:, :, None] + WIN_W - 1, 0, 2 * WIN_W - 2)
    return kcol, valid, dc_idx


def neighbourhood_attention(q, k, v, k_ctx, v_ctx, rpb):
    B, T, H, Dh = q.shape
    rows = T // GRID_W
    kh = min(WIN_H, rows)
    ncb = GRID_W // COL_QBLK
    scale = Dh ** -0.5
    kcol, valid, dc_idx = _col_tables()
    qg = q.reshape(B, rows, ncb, COL_QBLK, H, Dh)
    kg = k.reshape(B, rows, GRID_W, H, Dh)
    vg = v.reshape(B, rows, GRID_W, H, Dh)
    col_bias = rpb[:, :, dc_idx].astype(jnp.float32)
    mask = valid[:, :, None, :]
    Lc = k_ctx.shape[1]

    def one_row(args):
        r, q_row = args
        rs = jnp.clip(r - kh // 2, 0, rows - kh)
        k_slab = lax.dynamic_slice_in_dim(kg, rs, kh, axis=1)
        v_slab = lax.dynamic_slice_in_dim(vg, rs, kh, axis=1)
        k_blk = k_slab[:, :, kcol]
        v_blk = v_slab[:, :, kcol]
        dr_idx = rs + jnp.arange(kh) - r + (WIN_H - 1)
        bias = jnp.transpose(col_bias[:, dr_idx], (0, 2, 3, 1, 4))
        s_loc = jnp.einsum('bnqhd,bjnkhd->bhnqjk', q_row, k_blk).astype(jnp.float32) * scale + bias[None]
        s_loc = jnp.where(mask, s_loc, -jnp.inf).reshape(B, H, ncb, COL_QBLK, kh * COL_KBLK)
        s_ctx = jnp.einsum('bnqhd,bchd->bhnqc', q_row, k_ctx).astype(jnp.float32) * scale
        p = jax.nn.softmax(jnp.concatenate([s_loc, s_ctx], axis=-1), axis=-1).astype(v.dtype)
        p_loc = p[..., :kh * COL_KBLK].reshape(B, H, ncb, COL_QBLK, kh, COL_KBLK)
        p_ctx = p[..., kh * COL_KBLK:]
        return (jnp.einsum('bhnqjk,bjnkhd->bnqhd', p_loc, v_blk)
                + jnp.einsum('bhnqc,bchd->bnqhd', p_ctx, v_ctx))

    o = lax.map(one_row, (jnp.arange(rows), jnp.moveaxis(qg, 1, 0)))
    return jnp.moveaxis(o, 0, 1).reshape(B, T, H, Dh)


def multiscale_pool(h, w_pool, b_pool, pool_scale):
    B, L, D = h.shape
    hf = h.astype(jnp.float32)
    csum = jnp.concatenate([jnp.zeros((B, 1, D), jnp.float32), lax.cumsum(hf, axis=1)], axis=1)
    t = np.arange(L)
    outs = []
    for g, w in enumerate(POOL_WINDOWS):
        lo = np.clip(t - w // 2, 0, L)
        hi = np.clip(t + w // 2, 0, L)
        cnt = (hi - lo).astype(np.float32)[None, :, None]
        cg = csum[:, :, g * POOL_GROUP:(g + 1) * POOL_GROUP]
        outs.append((cg[:, hi] - cg[:, lo]) / cnt)
    pooled = jnp.concatenate(outs, axis=-1)
    mixed = (pooled - hf).astype(h.dtype).reshape(B, L, N_POOL_GROUPS, POOL_GROUP)
    y = jnp.einsum('blgp,gpq->blgq', mixed, w_pool).reshape(B, L, D) + b_pool
    return y * pool_scale


def conformer_conv(h, w_pw1, b_pw1, w_dw, b_dw, ln_g, ln_b, w_pw2, b_pw2):
    a = h @ w_pw1 + b_pw1
    u = a[..., :D_MODEL] * jax.nn.sigmoid(a[..., D_MODEL:])
    u = lax.conv_general_dilated(u, w_dw[:, None, :], window_strides=(1,),
                                 padding=[(CONV_WIDTH // 2, CONV_WIDTH // 2)],
                                 dimension_numbers=('NWC', 'WIO', 'NWC'),
                                 feature_group_count=D_MODEL) + b_dw
    uf = u.astype(jnp.float32)
    mu = jnp.mean(uf, axis=-1, keepdims=True)
    var = jnp.mean(jnp.square(uf - mu), axis=-1, keepdims=True)
    un = ((uf - mu) * lax.rsqrt(var + EPS)).astype(h.dtype) * ln_g + ln_b
    return jax.nn.silu(un) @ w_pw2 + b_pw2


def sq_relu_mlp(h, w_up, w_down):
    return jnp.square(jax.nn.relu(h @ w_up)) @ w_down


def setup_inputs(seed: int = 0) -> dict:
    key = jax.random.key(seed)
    keys = iter(jax.random.split(key, 128))

    def normal(shape, std=1.0):
        return jax.random.normal(next(keys), shape, jnp.float32) * std

    def gain(n):
        return 1.0 + 0.1 * normal((n,))

    def bias(n):
        return 0.01 * normal((n,))

    D = D_MODEL
    inp = {}
    inp['x_prompt'] = normal((BATCH, SEQ, D))
    inp['x_sample'] = normal((DEC_BATCH, DEC_SEQ, D))
    for i in range(DEPTH):
        if i % N_MIXERS == 0:
            inp['cache_k_l%d' % i] = normal((DEC_BATCH, PAST_LEN, N_HEADS, HEAD_DIM))
            inp['cache_v_l%d' % i] = normal((DEC_BATCH, PAST_LEN, N_HEADS, HEAD_DIM))
    inp['c'] = normal((DEC_BATCH, D))
    inp['c_ctx'] = normal((D,))
    for i in range(DEPTH):
        p = 'l%d_' % i
        inp[p + 'w_ada'] = normal((D, N_MOD * D), 0.5 * D ** -0.5)
        inp[p + 'b_ada'] = bias(N_MOD * D)
        inp[p + 'g_mix'] = gain(D)
        kind = i % N_MIXERS
        if kind == 0:
            inp[p + 'w_qkv'] = normal((D, 3 * D), D ** -0.5)
            inp[p + 'w_o'] = normal((D, D), D ** -0.5)
            inp[p + 'rpb'] = normal((N_HEADS, 2 * WIN_H - 1, 2 * WIN_W - 1), 0.2)
        elif kind == 1:
            inp[p + 'w_pool'] = normal((N_POOL_GROUPS, POOL_GROUP, POOL_GROUP), POOL_GROUP ** -0.5)
            inp[p + 'b_pool'] = bias(D)
            inp[p + 'pool_scale'] = gain(D)
        else:
            inp[p + 'w_pw1'] = normal((D, 2 * D), D ** -0.5)
            inp[p + 'b_pw1'] = bias(2 * D)
            inp[p + 'w_dw'] = normal((CONV_WIDTH, D), CONV_WIDTH ** -0.5)
            inp[p + 'b_dw'] = bias(D)
            inp[p + 'ln_g'] = gain(D)
            inp[p + 'ln_b'] = bias(D)
            inp[p + 'w_pw2'] = normal((D, D), D ** -0.5)
            inp[p + 'b_pw2'] = bias(D)
        inp[p + 'g_mlp'] = gain(D)
        inp[p + 'w_up'] = normal((D, FFN_DIM), D ** -0.5)
        inp[p + 'w_down'] = normal((FFN_DIM, D), FFN_DIM ** -0.5)
    inp['g_final'] = gain(D)
    return inp


def reference(x_prompt, x_sample, cache_k_l0, cache_v_l0, cache_k_l3, cache_v_l3, c, c_ctx,
              l0_w_ada, l0_b_ada, l0_g_mix, l0_w_qkv, l0_w_o, l0_rpb, l0_g_mlp, l0_w_up, l0_w_down,
              l1_w_ada, l1_b_ada, l1_g_mix, l1_w_pool, l1_b_pool, l1_pool_scale, l1_g_mlp, l1_w_up, l1_w_down,
              l2_w_ada, l2_b_ada, l2_g_mix, l2_w_pw1, l2_b_pw1, l2_w_dw, l2_b_dw, l2_ln_g, l2_ln_b,
              l2_w_pw2, l2_b_pw2, l2_g_mlp, l2_w_up, l2_w_down,
              l3_w_ada, l3_b_ada, l3_g_mix, l3_w_qkv, l3_w_o, l3_rpb, l3_g_mlp, l3_w_up, l3_w_down,
              g_final):
    common = [
        (l0_w_ada, l0_b_ada, l0_g_mix, l0_g_mlp, l0_w_up, l0_w_down),
        (l1_w_ada, l1_b_ada, l1_g_mix, l1_g_mlp, l1_w_up, l1_w_down),
        (l2_w_ada, l2_b_ada, l2_g_mix, l2_g_mlp, l2_w_up, l2_w_down),
        (l3_w_ada, l3_b_ada, l3_g_mix, l3_g_mlp, l3_w_up, l3_w_down),
    ]
    mixer_params = [
        (l0_w_qkv, l0_w_o, l0_rpb),
        (l1_w_pool, l1_b_pool, l1_pool_scale),
        (l2_w_pw1, l2_b_pw1, l2_w_dw, l2_b_dw, l2_ln_g, l2_ln_b, l2_w_pw2, l2_b_pw2),
        (l3_w_qkv, l3_w_o, l3_rpb),
    ]
    caches = {0: (cache_k_l0, cache_v_l0), 3: (cache_k_l3, cache_v_l3)}
    new_k, new_v = {}, {}

    hp, hs = x_prompt, x_sample
    for i in range(DEPTH):
        w_ada, b_ada, g_mix, g_mlp, w_up, w_down = common[i]
        sh_p, sc_p, gt_p, sh2_p, sc2_p, gt2_p = adaln(c_ctx[None], w_ada, b_ada)
        sh_s, sc_s, gt_s, sh2_s, sc2_s, gt2_s = adaln(c, w_ada, b_ada)
        up = modulate(rmsnorm(hp, g_mix), sh_p, sc_p)
        us = modulate(rmsnorm(hs, g_mix), sh_s, sc_s)
        kind = i % N_MIXERS
        if kind == 0:
            w_qkv, w_o, rpb = mixer_params[i]
            qp, kp, vp = split_qkv(up, w_qkv)
            op = ctx_self_attention(qp, kp, vp)
            new_k[i], new_v[i] = kp, vp
            qs, ks, vs = split_qkv(us, w_qkv)
            ck, cv = caches[i]
            os_ = neighbourhood_attention(qs, ks, vs, ck, cv, rpb)
            mp = op.reshape(hp.shape) @ w_o
            ms = os_.reshape(hs.shape) @ w_o
        elif kind == 1:
            mp = multiscale_pool(up, *mixer_params[i])
            ms = multiscale_pool(us, *mixer_params[i])
        else:
            mp = conformer_conv(up, *mixer_params[i])
            ms = conformer_conv(us, *mixer_params[i])
        hp = hp + gt_p * mp
        hs = hs + gt_s * ms
        hp = hp + gt2_p * sq_relu_mlp(modulate(rmsnorm(hp, g_mlp), sh2_p, sc2_p), w_up, w_down)
        hs = hs + gt2_s * sq_relu_mlp(modulate(rmsnorm(hs, g_mlp), sh2_s, sc2_s), w_up, w_down)

    y_prompt = rmsnorm(hp, g_final)
    y_sample = rmsnorm(hs, g_final)
    return (y_prompt, y_sample, new_k[0], new_v[0], new_k[3], new_v[3])
```

```python
import functools

import numpy as np
import jax
import jax.numpy as jnp
from jax import lax
from jax.experimental import pallas as pl
from jax.experimental.pallas import tpu as pltpu

D = 4096
N_HEADS = 32
HEAD_DIM = 128
FFN = 4 * D
SEQ = 256
DEC_SEQ = 1024
GRID_W = 64
GRID_H = DEC_SEQ // GRID_W
PAST = 512
WIN_H = 8
WIN_W = 16
CONV_W = 31
POOL_WINDOWS = (2, 4, 8, 16)
POOL_GROUP = D // len(POOL_WINDOWS)
N_MOD = 6
EPS = 1e-6
T_CTX = 8192
T_LAT = 8192
T_ALL = T_CTX + T_LAT
GROUP = 1024
N_GROUPS = T_ALL // GROUP
N_CTX_GROUPS = T_CTX // GROUP
MOD_ROWS = 8
MASKED = -1e30

LANES = 128
SUBLANES = 8
BF16_ROWS = 16
VMEM_PHYSICAL_BYTES = 64 * 2**20
VMEM_REQUEST_CAP_BYTES = 60 * 2**20
VMEM_COMPILER_MARGIN_BYTES = 6 * 2**20

F32 = jnp.float32
BF16 = jnp.bfloat16


def _params(block_bytes, scratch_bytes=0, semantics=None):
    need = 2 * block_bytes + scratch_bytes + VMEM_COMPILER_MARGIN_BYTES
    return pltpu.CompilerParams(
        dimension_semantics=semantics,
        vmem_limit_bytes=int(min(max(need, 16 * 2**20), VMEM_REQUEST_CAP_BYTES)))


def _nbytes(shape, dtype):
    return int(np.prod(shape)) * jnp.dtype(dtype).itemsize


def _rows(v, n):
    return jnp.broadcast_to(v, (n, v.shape[-1]))


def _ada_kernel(c_ref, w_ref, b_ref, o_ref):
    c = c_ref[...]
    s = (c * jax.nn.sigmoid(c)).astype(BF16)
    o_ref[...] = jnp.dot(s, w_ref[...].astype(BF16), preferred_element_type=F32) + b_ref[...]


def _adaln(cond16, w_ada, b_ada):
    tn = 512
    n = w_ada.shape[1]
    blocks = _nbytes((16, D), F32) + _nbytes((D, tn), F32) + _nbytes((1, tn), F32) + _nbytes((16, tn), F32)
    return pl.pallas_call(
        _ada_kernel,
        grid=(n // tn,),
        in_specs=[pl.BlockSpec((16, D), lambda j: (0, 0)),
                  pl.BlockSpec((D, tn), lambda j: (0, j)),
                  pl.BlockSpec((1, tn), lambda j: (0, j))],
        out_specs=pl.BlockSpec((16, tn), lambda j: (0, j)),
        out_shape=jax.ShapeDtypeStruct((16, n), F32),
        compiler_params=_params(blocks, _nbytes((D, tn), BF16), ("arbitrary",)),
        name="adaln",
    )(cond16, w_ada, b_ada.reshape(1, n))


def _group_modulation(c, c_ctx, w_ada, b_ada):
    cond = jnp.concatenate([c_ctx[None], c, jnp.zeros((16 - 1 - c.shape[0], D), F32)], axis=0)
    m = _adaln(cond, w_ada, b_ada).reshape(16, N_MOD, D)
    rows = np.concatenate([np.zeros(N_CTX_GROUPS, np.int32), 1 + np.arange(N_GROUPS - N_CTX_GROUPS, dtype=np.int32)])
    m = m[rows]
    return jnp.concatenate([m, jnp.zeros((N_GROUPS, MOD_ROWS - N_MOD, D), F32)], axis=1)


def _rms_mod_prologue(x_ref, g_ref, mod_ref, shift_row, pb_ref, xn_ref):
    rc = BF16_ROWS
    pb_ref[0] = _rows(g_ref[...], rc)
    pb_ref[1] = _rows(1.0 + mod_ref[shift_row + 1:shift_row + 2, :], rc)
    pb_ref[2] = _rows(mod_ref[shift_row:shift_row + 1, :], rc)

    def body(r, carry):
        rows = pl.ds(pl.multiple_of(r * rc, rc), rc)
        x = x_ref[rows, :]
        inv = lax.rsqrt(jnp.mean(x * x, axis=-1, keepdims=True) + EPS)
        xn_ref[rows, :] = (((x * inv) * pb_ref[0]) * pb_ref[1] + pb_ref[2]).astype(xn_ref.dtype)
        return carry

    lax.fori_loop(0, x_ref.shape[0] // rc, body, 0)


def _ln_silu_prologue(x_ref, g_ref, b_ref, pb_ref, xn_ref):
    rc = BF16_ROWS
    pb_ref[0] = _rows(g_ref[...], rc)
    pb_ref[1] = _rows(b_ref[...], rc)

    def body(r, carry):
        rows = pl.ds(pl.multiple_of(r * rc, rc), rc)
        x = x_ref[rows, :]
        xc = x - jnp.mean(x, axis=-1, keepdims=True)
        inv = lax.rsqrt(jnp.mean(xc * xc, axis=-1, keepdims=True) + EPS)
        un = (xc * inv) * pb_ref[0] + pb_ref[1]
        xn_ref[rows, :] = (un * jax.nn.sigmoid(un)).astype(xn_ref.dtype)
        return carry

    lax.fori_loop(0, x_ref.shape[0] // rc, body, 0)


def _gated_residual_rows(x_ref, gate_row_value, pb_ref, o_ref):
    rc = BF16_ROWS
    pb_ref[0] = _rows(gate_row_value, rc)

    def body(r, carry):
        rows = pl.ds(pl.multiple_of(r * rc, rc), rc)
        o_ref[rows, :] = x_ref[rows, :] + pb_ref[0] * o_ref[rows, :]
        return carry

    lax.fori_loop(0, x_ref.shape[0] // rc, body, 0)


def _group_of(tile, tm):
    return (tile * tm) // GROUP


def _qkv_ctx_kernel(x_ref, g_ref, mod_ref, w_ref, q_ref, k_ref, v_ref, xn_ref, pb_ref, *, n_each):
    j = pl.program_id(1)

    @pl.when(j == 0)
    def _():
        _rms_mod_prologue(x_ref, g_ref, mod_ref, 0, pb_ref, xn_ref)

    def proj():
        return jnp.dot(xn_ref[...], w_ref[...], preferred_element_type=F32)

    @pl.when(j < n_each)
    def _():
        q_ref[...] = proj().astype(q_ref.dtype)

    @pl.when(jnp.logical_and(j >= n_each, j < 2 * n_each))
    def _():
        k_ref[...] = proj()

    @pl.when(j >= 2 * n_each)
    def _():
        v_ref[...] = proj()


def _qkv_ctx(h, g, modg, w_qkv):
    tm, tn = 512, 1024
    n_each = D // tn

    def park(lo):
        return lambda i, j: (i, jnp.clip(j - lo, 0, n_each - 1))

    blocks = (_nbytes((tm, D), F32) + _nbytes((1, D), F32) + _nbytes((MOD_ROWS, D), F32) + _nbytes((D, tn), BF16)
              + _nbytes((tm, tn), BF16) + 2 * _nbytes((tm, tn), F32))
    scratch = _nbytes((tm, D), BF16) + _nbytes((3, BF16_ROWS, D), F32)
    return pl.pallas_call(
        functools.partial(_qkv_ctx_kernel, n_each=n_each),
        grid=(T_CTX // tm, 3 * n_each),
        in_specs=[pl.BlockSpec((tm, D), lambda i, j: (i, 0)),
                  pl.BlockSpec((1, D), lambda i, j: (0, 0)),
                  pl.BlockSpec((None, MOD_ROWS, D), lambda i, j: (_group_of(i, tm), 0, 0)),
                  pl.BlockSpec((D, tn), lambda i, j: (0, j))],
        out_specs=[pl.BlockSpec((tm, tn), park(0)),
                   pl.BlockSpec((tm, tn), park(n_each)),
                   pl.BlockSpec((tm, tn), park(2 * n_each))],
        out_shape=[jax.ShapeDtypeStruct((T_CTX, D), BF16),
                   jax.ShapeDtypeStruct((T_CTX, D), F32),
                   jax.ShapeDtypeStruct((T_CTX, D), F32)],
        scratch_shapes=[pltpu.VMEM((tm, D), BF16), pltpu.VMEM((3, BF16_ROWS, D), F32)],
        compiler_params=_params(blocks, scratch, ("arbitrary", "arbitrary")),
        name="qkv_ctx",
    )(h, g.reshape(1, D), modg, w_qkv)


def _qkv_lat_kernel(x_ref, g_ref, mod_ref, w_ref, o_ref, xn_ref, pb_ref):
    @pl.when(pl.program_id(1) == 0)
    def _():
        _rms_mod_prologue(x_ref, g_ref, mod_ref, 0, pb_ref, xn_ref)

    o_ref[...] = jnp.dot(xn_ref[...], w_ref[...], preferred_element_type=F32).astype(o_ref.dtype)


def _qkv_lat(h, g, modg, w_qkv):
    tm, tn = 512, 1024
    n_each = D // tn
    first = T_CTX // tm
    blocks = (_nbytes((tm, D), F32) + _nbytes((1, D), F32) + _nbytes((MOD_ROWS, D), F32) + _nbytes((D, tn), BF16)
              + _nbytes((tm, tn), BF16))
    scratch = _nbytes((tm, D), BF16) + _nbytes((3, BF16_ROWS, D), F32)
    return pl.pallas_call(
        _qkv_lat_kernel,
        grid=(T_LAT // tm, 3 * n_each),
        in_specs=[pl.BlockSpec((tm, D), lambda i, j: (first + i, 0)),
                  pl.BlockSpec((1, D), lambda i, j: (0, 0)),
                  pl.BlockSpec((None, MOD_ROWS, D), lambda i, j: (_group_of(first + i, tm), 0, 0)),
                  pl.BlockSpec((D, tn), lambda i, j: (0, j))],
        out_specs=pl.BlockSpec((None, tm, tn), lambda i, j: (j // n_each, i, j % n_each)),
        out_shape=jax.ShapeDtypeStruct((3, T_LAT, D), BF16),
        scratch_shapes=[pltpu.VMEM((tm, D), BF16), pltpu.VMEM((3, BF16_ROWS, D), F32)],
        compiler_params=_params(blocks, scratch, ("arbitrary", "arbitrary")),
        name="qkv_lat",
    )(h, g.reshape(1, D), modg, w_qkv)


def _glu_kernel(x_ref, g_ref, mod_ref, wa_ref, wb_ref, ba_ref, bb_ref, o_ref, xn_ref, pb_ref):
    @pl.when(pl.program_id(1) == 0)
    def _():
        _rms_mod_prologue(x_ref, g_ref, mod_ref, 0, pb_ref, xn_ref)

    a = jnp.dot(xn_ref[...], wa_ref[...], preferred_element_type=F32) + ba_ref[...]
    b = jnp.dot(xn_ref[...], wb_ref[...], preferred_element_type=F32) + bb_ref[...]
    o_ref[...] = a * jax.nn.sigmoid(b)


def _pw1_glu(h, g, modg, w_pw1, b_pw1):
    tm, tn = 512, 512
    nj = D // tn
    blocks = (_nbytes((tm, D), F32) + _nbytes((1, D), F32) + _nbytes((MOD_ROWS, D), F32) + 2 * _nbytes((D, tn), BF16)
              + 2 * _nbytes((1, tn), F32) + _nbytes((tm, tn), F32))
    scratch = _nbytes((tm, D), BF16) + _nbytes((3, BF16_ROWS, D), F32)
    b2 = b_pw1.reshape(1, 2 * D)
    return pl.pallas_call(
        _glu_kernel,
        grid=(T_ALL // tm, nj),
        in_specs=[pl.BlockSpec((tm, D), lambda i, j: (i, 0)),
                  pl.BlockSpec((1, D), lambda i, j: (0, 0)),
                  pl.BlockSpec((None, MOD_ROWS, D), lambda i, j: (_group_of(i, tm), 0, 0)),
                  pl.BlockSpec((D, tn), lambda i, j: (0, j)),
                  pl.BlockSpec((D, tn), lambda i, j: (0, nj + j)),
                  pl.BlockSpec((1, tn), lambda i, j: (0, j)),
                  pl.BlockSpec((1, tn), lambda i, j: (0, nj + j))],
        out_specs=pl.BlockSpec((tm, tn), lambda i, j: (i, j)),
        out_shape=jax.ShapeDtypeStruct((T_ALL, D), F32),
        scratch_shapes=[pltpu.VMEM((tm, D), BF16), pltpu.VMEM((3, BF16_ROWS, D), F32)],
        compiler_params=_params(blocks, scratch, ("arbitrary", "arbitrary")),
        name="pw1_glu",
    )(h, g.reshape(1, D), modg, w_pw1, w_pw1, b2, b2)


def _linear_res_kernel(x_ref, w_ref, res_ref, mod_ref, o_ref, *, gate_row):
    y = jnp.dot(x_ref[...], w_ref[...], preferred_element_type=F32)
    o_ref[...] = res_ref[...] + mod_ref[gate_row:gate_row + 1, :] * y


def _linear_res(x, w, res, modg, gate_row):
    tm, tn = 1024, 512
    k = x.shape[1]
    n = w.shape[1]
    blocks = (_nbytes((tm, k), BF16) + _nbytes((k, tn), BF16) + 2 * _nbytes((tm, tn), F32)
              + _nbytes((MOD_ROWS, tn), F32))
    return pl.pallas_call(
        functools.partial(_linear_res_kernel, gate_row=gate_row),
        grid=(T_ALL // tm, n // tn),
        in_specs=[pl.BlockSpec((tm, k), lambda i, j: (i, 0)),
                  pl.BlockSpec((k, tn), lambda i, j: (0, j)),
                  pl.BlockSpec((tm, tn), lambda i, j: (i, j)),
                  pl.BlockSpec((None, MOD_ROWS, tn), lambda i, j: (_group_of(i, tm), 0, j))],
        out_specs=pl.BlockSpec((tm, tn), lambda i, j: (i, j)),
        out_shape=jax.ShapeDtypeStruct((T_ALL, n), F32),
        compiler_params=_params(blocks, 0, ("arbitrary", "arbitrary")),
        name="linear_res",
    )(x, w, res, modg)


def _pool_linear_res_kernel(x_ref, w_ref, b_ref, s_ref, res_ref, mod_ref, o_ref):
    y = (jnp.dot(x_ref[...], w_ref[...], preferred_element_type=F32) + b_ref[...]) * s_ref[...]
    o_ref[...] = res_ref[...] + mod_ref[2:3, :] * y


def _pool_linear_res(mixed, w_pool, b_pool, pool_scale, res, modg):
    tm, pg = 1024, POOL_GROUP
    blocks = (_nbytes((tm, pg), BF16) + _nbytes((pg, pg), BF16) + 2 * _nbytes((1, pg), F32)
              + 2 * _nbytes((tm, pg), F32) + _nbytes((MOD_ROWS, pg), F32))
    return pl.pallas_call(
        _pool_linear_res_kernel,
        grid=(T_ALL // tm, len(POOL_WINDOWS)),
        in_specs=[pl.BlockSpec((tm, pg), lambda i, j: (i, j)),
                  pl.BlockSpec((None, pg, pg), lambda i, j: (j, 0, 0)),
                  pl.BlockSpec((1, pg), lambda i, j: (0, j)),
                  pl.BlockSpec((1, pg), lambda i, j: (0, j)),
                  pl.BlockSpec((tm, pg), lambda i, j: (i, j)),
                  pl.BlockSpec((None, MOD_ROWS, pg), lambda i, j: (_group_of(i, tm), 0, j))],
        out_specs=pl.BlockSpec((tm, pg), lambda i, j: (i, j)),
        out_shape=jax.ShapeDtypeStruct((T_ALL, D), F32),
        compiler_params=_params(blocks, 0, ("arbitrary", "arbitrary")),
        name="pool_linear_res",
    )(mixed, w_pool, b_pool.reshape(1, D), pool_scale.reshape(1, D), res, modg)


def _ln_linear_res_kernel(x_ref, lg_ref, lb_ref, w_ref, b_ref, res_ref, mod_ref, o_ref, xn_ref, pb_ref):
    @pl.when(pl.program_id(1) == 0)
    def _():
        _ln_silu_prologue(x_ref, lg_ref, lb_ref, pb_ref, xn_ref)

    y = jnp.dot(xn_ref[...], w_ref[...], preferred_element_type=F32) + b_ref[...]
    o_ref[...] = res_ref[...] + mod_ref[2:3, :] * y


def _ln_linear_res(u, ln_g, ln_b, w, b, res, modg):
    tm, tn = 512, 1024
    blocks = (_nbytes((tm, D), F32) + 2 * _nbytes((1, D), F32) + _nbytes((D, tn), BF16) + _nbytes((1, tn), F32)
              + 2 * _nbytes((tm, tn), F32) + _nbytes((MOD_ROWS, tn), F32))
    scratch = _nbytes((tm, D), BF16) + _nbytes((3, BF16_ROWS, D), F32)
    return pl.pallas_call(
        _ln_linear_res_kernel,
        grid=(T_ALL // tm, D // tn),
        in_specs=[pl.BlockSpec((tm, D), lambda i, j: (i, 0)),
                  pl.BlockSpec((1, D), lambda i, j: (0, 0)),
                  pl.BlockSpec((1, D), lambda i, j: (0, 0)),
                  pl.BlockSpec((D, tn), lambda i, j: (0, j)),
                  pl.BlockSpec((1, tn), lambda i, j: (0, j)),
                  pl.BlockSpec((tm, tn), lambda i, j: (i, j)),
                  pl.BlockSpec((None, MOD_ROWS, tn), lambda i, j: (_group_of(i, tm), 0, j))],
        out_specs=pl.BlockSpec((tm, tn), lambda i, j: (i, j)),
        out_shape=jax.ShapeDtypeStruct((T_ALL, D), F32),
        scratch_shapes=[pltpu.VMEM((tm, D), BF16), pltpu.VMEM((3, BF16_ROWS, D), F32)],
        compiler_params=_params(blocks, scratch, ("arbitrary", "arbitrary")),
        name="ln_linear_res",
    )(u, ln_g.reshape(1, D), ln_b.reshape(1, D), w, b.reshape(1, D), res, modg)


def _mlp_kernel(x_ref, g_ref, mod_ref, wup_ref, wdn_ref, o_ref, xn_ref, pb_ref):
    f = pl.program_id(1)

    @pl.when(f == 0)
    def _():
        _rms_mod_prologue(x_ref, g_ref, mod_ref, 3, pb_ref, xn_ref)
        o_ref[...] = jnp.zeros_like(o_ref)

    up = jnp.dot(xn_ref[...], wup_ref[...], preferred_element_type=F32)
    act = jnp.square(jnp.maximum(up, 0.0)).astype(BF16)
    o_ref[...] += jnp.dot(act, wdn_ref[...], preferred_element_type=F32)

    @pl.when(f == pl.num_programs(1) - 1)
    def _():
        _gated_residual_rows(x_ref, mod_ref[5:6, :], pb_ref, o_ref)


def _mlp(h, g, modg, w_up, w_down):
    tm, tf = 512, 512
    blocks = (_nbytes((tm, D), F32) + _nbytes((1, D), F32) + _nbytes((MOD_ROWS, D), F32) + _nbytes((D, tf), BF16)
              + _nbytes((tf, D), BF16) + _nbytes((tm, D), F32))
    scratch = _nbytes((tm, D), BF16) + _nbytes((3, BF16_ROWS, D), F32)
    return pl.pallas_call(
        _mlp_kernel,
        grid=(T_ALL // tm, FFN // tf),
        in_specs=[pl.BlockSpec((tm, D), lambda i, f: (i, 0)),
                  pl.BlockSpec((1, D), lambda i, f: (0, 0)),
                  pl.BlockSpec((None, MOD_ROWS, D), lambda i, f: (_group_of(i, tm), 0, 0)),
                  pl.BlockSpec((D, tf), lambda i, f: (0, f)),
                  pl.BlockSpec((tf, D), lambda i, f: (f, 0))],
        out_specs=pl.BlockSpec((tm, D), lambda i, f: (i, 0)),
        out_shape=jax.ShapeDtypeStruct((T_ALL, D), F32),
        scratch_shapes=[pltpu.VMEM((tm, D), BF16), pltpu.VMEM((3, BF16_ROWS, D), F32)],
        compiler_params=_params(blocks, scratch, ("arbitrary", "arbitrary")),
        name="mlp",
    )(h, g.reshape(1, D), modg, w_up, w_down)


def _dot_nt(a, b):
    return lax.dot_general(a, b, (((1,), (1,)), ((), ())), preferred_element_type=F32)


def _ctx_attn_kernel(q_ref, k_ref, v_ref, o_ref):
    scale = HEAD_DIM ** -0.5
    for hd in range(N_HEADS):
        cols = slice(hd * HEAD_DIM, (hd + 1) * HEAD_DIM)
        q = q_ref[:, cols]
        k = k_ref[:, cols].astype(BF16)
        v = v_ref[:, cols].astype(BF16)
        s = _dot_nt(q, k) * scale
        p = jnp.exp(s - jnp.max(s, axis=-1, keepdims=True))
        denom = jnp.sum(p, axis=-1, keepdims=True)
        o = jnp.dot(p.astype(BF16), v, preferred_element_type=F32) / denom
        o_ref[:, cols] = o.astype(o_ref.dtype)


def _ctx_attention(q, k, v):
    blocks = 2 * _nbytes((SEQ, D), BF16) + 2 * _nbytes((SEQ, D), F32)
    return pl.pallas_call(
        _ctx_attn_kernel,
        grid=(T_CTX // SEQ,),
        in_specs=[pl.BlockSpec((SEQ, D), lambda b: (b, 0))] * 3,
        out_specs=pl.BlockSpec((SEQ, D), lambda b: (b, 0)),
        out_shape=jax.ShapeDtypeStruct((T_CTX, D), BF16),
        compiler_params=_params(blocks, 0, ("arbitrary",)),
        name="ctx_attention",
    )(q, k, v)


Q_CHUNK_ROWS = 4


def _row_start(r):
    return min(max(r - WIN_H // 2, 0), GRID_H - WIN_H)


def _chunk_key_rows(c):
    r0, r1 = c * Q_CHUNK_ROWS, (c + 1) * Q_CHUNK_ROWS - 1
    lo, hi = _row_start(r0), _row_start(r1) + WIN_H
    lo -= lo % 2
    hi += hi % 2
    return lo, hi


def _natten_kernel(q_ref, k_ref, v_ref, kc_ref, vc_ref, tab_ref, o_ref, bias_ref):
    scale = HEAD_DIM ** -0.5

    @pl.when(pl.program_id(1) == 0)
    def _():
        left = lax.broadcasted_iota(jnp.int32, (GRID_W, LANES), 1) < GRID_W
        masked = jnp.full((GRID_W, LANES), MASKED, F32)
        for r in range(GRID_H):
            rs = _row_start(r)
            for pair in range(GRID_H // 2):
                halves = []
                for kr in (2 * pair, 2 * pair + 1):
                    halves.append(tab_ref[kr - r + WIN_H - 1] if rs <= kr < rs + WIN_H else masked)
                bias_ref[r * GRID_W:(r + 1) * GRID_W, pair * LANES:(pair + 1) * LANES] = (
                    jnp.where(left, halves[0], halves[1]))

    kc = kc_ref[...].astype(BF16)
    vc = vc_ref[...].astype(BF16)
    nq = Q_CHUNK_ROWS * GRID_W
    for c in range(GRID_H // Q_CHUNK_ROWS):
        lo, hi = _chunk_key_rows(c)
        keys = slice(lo * GRID_W, hi * GRID_W)
        q = q_ref[c * nq:(c + 1) * nq, :]
        s_loc = _dot_nt(q, k_ref[keys, :]) * scale + bias_ref[c * nq:(c + 1) * nq, keys]
        s_ctx = _dot_nt(q, kc) * scale
        m = jnp.maximum(jnp.max(s_loc, axis=-1, keepdims=True), jnp.max(s_ctx, axis=-1, keepdims=True))
        p_loc = jnp.exp(s_loc - m)
        p_ctx = jnp.exp(s_ctx - m)
        denom = jnp.sum(p_loc, axis=-1, keepdims=True) + jnp.sum(p_ctx, axis=-1, keepdims=True)
        o = (jnp.dot(p_loc.astype(BF16), v_ref[keys, :], preferred_element_type=F32)
             + jnp.dot(p_ctx.astype(BF16), vc, preferred_element_type=F32)) / denom
        o_ref[c * nq:(c + 1) * nq, :] = o.astype(o_ref.dtype)


def _column_bias_tables(rpb):
    cols = np.arange(GRID_W)
    start = np.clip(cols - WIN_W // 2, 0, GRID_W - WIN_W)
    kc = cols[None, :]
    valid = (kc >= start[:, None]) & (kc < start[:, None] + WIN_W)
    dc = np.clip(kc - cols[:, None] + WIN_W - 1, 0, 2 * WIN_W - 2)
    tab = jnp.where(valid[None, None], rpb[:, :, dc], MASKED)
    return jnp.concatenate([tab, tab], axis=-1)


def _natten(qkv, cache_k, cache_v, rpb):
    tab = _column_bias_tables(rpb)
    n_dr = 2 * WIN_H - 1
    ck = cache_k.reshape(T_LAT // DEC_SEQ, PAST, D)
    cv = cache_v.reshape(T_LAT // DEC_SEQ, PAST, D)
    blocks = (4 * _nbytes((DEC_SEQ, HEAD_DIM), BF16) + 2 * _nbytes((PAST, HEAD_DIM), F32)
              + _nbytes((n_dr, GRID_W, LANES), F32))
    scratch = _nbytes((DEC_SEQ, DEC_SEQ), F32)
    return pl.pallas_call(
        _natten_kernel,
        grid=(N_HEADS, T_LAT // DEC_SEQ),
        in_specs=[pl.BlockSpec((None, DEC_SEQ, HEAD_DIM), lambda hd, b: (0, b, hd)),
                  pl.BlockSpec((None, DEC_SEQ, HEAD_DIM), lambda hd, b: (1, b, hd)),
                  pl.BlockSpec((None, DEC_SEQ, HEAD_DIM), lambda hd, b: (2, b, hd)),
                  pl.BlockSpec((None, PAST, HEAD_DIM), lambda hd, b: (b, 0, hd)),
                  pl.BlockSpec((None, PAST, HEAD_DIM), lambda hd, b: (b, 0, hd)),
                  pl.BlockSpec((None, n_dr, GRID_W, LANES), lambda hd, b: (hd, 0, 0, 0))],
        out_specs=pl.BlockSpec((DEC_SEQ, HEAD_DIM), lambda hd, b: (b, hd)),
        out_shape=jax.ShapeDtypeStruct((T_LAT, D), BF16),
        scratch_shapes=[pltpu.VMEM((DEC_SEQ, DEC_SEQ), F32)],
        compiler_params=_params(blocks, scratch, ("arbitrary", "arbitrary")),
        name="natten",
    )(qkv, qkv, qkv, ck, cv, tab)


def _shift_rows(y, s, pos, seq_len):
    rolled = pltpu.roll(y, s % y.shape[0], axis=0)
    src = pos - s
    return jnp.where(jnp.logical_and(src >= 0, src < seq_len), rolled, 0.0)


def _window_sum(x, half, pos, seq_len):
    back = _shift_rows(x, 1, pos, seq_len)
    fwd = x
    n = 1
    while n < half:
        back = back + _shift_rows(back, n, pos, seq_len)
        fwd = fwd + _shift_rows(fwd, -n, pos, seq_len)
        n *= 2
    return back + fwd


def _pool_mix_kernel(x_ref, g_ref, mod_ref, o_ref, inv_ref):
    seq_len = jnp.where(pl.program_id(0) < N_CTX_GROUPS, SEQ, DEC_SEQ)
    rc = BF16_ROWS

    def stats(r, carry):
        rows = pl.ds(pl.multiple_of(r * rc, rc), rc)
        x = x_ref[rows, :]
        inv = lax.rsqrt(jnp.mean(x * x, axis=-1, keepdims=True) + EPS)
        inv_ref[rows, :] = jnp.broadcast_to(inv, (rc, LANES))
        return carry

    lax.fori_loop(0, GROUP // rc, stats, 0)

    for grp, window in enumerate(POOL_WINDOWS):
        half = window // 2

        def block(blk, carry, grp=grp, half=half):
            cols = pl.ds(pl.multiple_of(grp * POOL_GROUP + blk * LANES, LANES), LANES)
            pos = lax.broadcasted_iota(jnp.int32, (GROUP, LANES), 0) & (seq_len - 1)
            cnt = (jnp.minimum(pos + half, seq_len) - jnp.maximum(pos - half, 0)).astype(F32)
            up = ((x_ref[:, cols] * inv_ref[...]) * g_ref[:, cols]) * (1.0 + mod_ref[1:2, cols]) + mod_ref[0:1, cols]
            pooled = _window_sum(up, half, pos, seq_len) / cnt
            o_ref[:, cols] = (pooled - up).astype(o_ref.dtype)
            return carry

        lax.fori_loop(0, POOL_GROUP // LANES, block, 0)


def _pool_mix(h, g, modg):
    blocks = _nbytes((GROUP, D), F32) + _nbytes((1, D), F32) + _nbytes((MOD_ROWS, D), F32) + _nbytes((GROUP, D), BF16)
    scratch = _nbytes((GROUP, LANES), F32)
    return pl.pallas_call(
        _pool_mix_kernel,
        grid=(N_GROUPS,),
        in_specs=[pl.BlockSpec((GROUP, D), lambda i: (i, 0)),
                  pl.BlockSpec((1, D), lambda i: (0, 0)),
                  pl.BlockSpec((None, MOD_ROWS, D), lambda i: (i, 0, 0))],
        out_specs=pl.BlockSpec((GROUP, D), lambda i: (i, 0)),
        out_shape=jax.ShapeDtypeStruct((T_ALL, D), BF16),
        scratch_shapes=[pltpu.VMEM((GROUP, LANES), F32)],
        compiler_params=_params(blocks, scratch, ("arbitrary",)),
        name="pool_mix",
    )(h, g.reshape(1, D), modg)


CONV_HALO = 16
CONV_BLOCK = 512
CONV_ROWS = 64


def _dwconv_kernel(u_ref, w_ref, b_ref, o_ref, pad_ref, *, seq_len):
    n_seq = GROUP // seq_len
    stride = seq_len + CONV_HALO
    zeros = jnp.zeros((CONV_HALO, LANES), F32)
    for lb in range(CONV_BLOCK // LANES):
        cols = slice(lb * LANES, (lb + 1) * LANES)
        for s in range(n_seq + 1):
            pad_ref[lb, s * stride:s * stride + CONV_HALO, :] = zeros
        for s in range(n_seq):
            top = CONV_HALO + s * stride
            pad_ref[lb, top:top + seq_len, :] = u_ref[s * seq_len:(s + 1) * seq_len, cols]
        bias = jnp.broadcast_to(b_ref[:, cols], (SUBLANES, LANES))[None]
        for s in range(n_seq):
            for ch in range(seq_len // CONV_ROWS):
                first = CONV_HALO + s * stride + ch * CONV_ROWS - CONV_W // 2
                acc = jnp.broadcast_to(bias, (CONV_ROWS // SUBLANES, SUBLANES, LANES))
                for tap in range(CONV_W):
                    x = pad_ref[lb, first + tap:first + tap + CONV_ROWS, :]
                    acc = acc + x.reshape(CONV_ROWS // SUBLANES, SUBLANES, LANES) * w_ref[tap, :, cols][None]
                out_row = s * seq_len + ch * CONV_ROWS
                o_ref[out_row:out_row + CONV_ROWS, cols] = acc.reshape(CONV_ROWS, LANES)


def _dwconv_part(u, w8, b, first_group, n_groups, seq_len):
    n_seq = GROUP // seq_len
    pad_rows = CONV_HALO + n_seq * (seq_len + CONV_HALO)
    ncb = D // CONV_BLOCK
    blocks = 2 * _nbytes((GROUP, CONV_BLOCK), F32) + _nbytes((CONV_W, SUBLANES, CONV_BLOCK), F32)
    scratch = _nbytes((CONV_BLOCK // LANES, pad_rows, LANES), F32)
    return pl.pallas_call(
        functools.partial(_dwconv_kernel, seq_len=seq_len),
        grid=(n_groups, ncb),
        in_specs=[pl.BlockSpec((GROUP, CONV_BLOCK), lambda i, j: (first_group + i, j)),
                  pl.BlockSpec((CONV_W, SUBLANES, CONV_BLOCK), lambda i, j: (0, 0, j)),
                  pl.BlockSpec((1, CONV_BLOCK), lambda i, j: (0, j))],
        out_specs=pl.BlockSpec((GROUP, CONV_BLOCK), lambda i, j: (i, j)),
        out_shape=jax.ShapeDtypeStruct((n_groups * GROUP, D), F32),
        scratch_shapes=[pltpu.VMEM((CONV_BLOCK // LANES, pad_rows, LANES), F32)],
        compiler_params=_params(blocks, scratch, ("arbitrary", "arbitrary")),
        name="dwconv_%d" % seq_len,
    )(u, w8, b.reshape(1, D))


def _dwconv(u, w_dw, b_dw):
    w8 = jnp.broadcast_to(w_dw[:, None, :], (CONV_W, SUBLANES, D))
    ctx = _dwconv_part(u, w8, b_dw, 0, N_CTX_GROUPS, SEQ)
    lat = _dwconv_part(u, w8, b_dw, N_CTX_GROUPS, N_GROUPS - N_CTX_GROUPS, DEC_SEQ)
    return jnp.concatenate([ctx, lat], axis=0)


def _final_norm_kernel(x_ref, g_ref, o_ref):
    x = x_ref[...]
    o_ref[...] = (x * lax.rsqrt(jnp.mean(x * x, axis=-1, keepdims=True) + EPS)) * g_ref[...]


def _final_norm(h, g, first_row, n_rows):
    tm = 256
    first = first_row // tm
    blocks = 2 * _nbytes((tm, D), F32) + _nbytes((1, D), F32)
    return pl.pallas_call(
        _final_norm_kernel,
        grid=(n_rows // tm,),
        in_specs=[pl.BlockSpec((tm, D), lambda i: (first + i, 0)),
                  pl.BlockSpec((1, D), lambda i: (0, 0))],
        out_specs=pl.BlockSpec((tm, D), lambda i: (i, 0)),
        out_shape=jax.ShapeDtypeStruct((n_rows, D), F32),
        compiler_params=_params(blocks, 0, ("arbitrary",)),
        name="final_norm",
    )(h, g.reshape(1, D))


def _attention_layer(h, modg, g_mix, w_qkv, w_o, rpb, cache_k, cache_v):
    w_qkv = w_qkv.astype(BF16)
    q_ctx, k_ctx, v_ctx = _qkv_ctx(h, g_mix, modg, w_qkv)
    o_ctx = _ctx_attention(q_ctx, k_ctx, v_ctx)
    qkv_lat = _qkv_lat(h, g_mix, modg, w_qkv)
    o_lat = _natten(qkv_lat, cache_k, cache_v, rpb)
    o = jnp.concatenate([o_ctx, o_lat], axis=0)
    h = _linear_res(o, w_o.astype(BF16), h, modg, 2)
    new_shape = (T_CTX // SEQ, SEQ, N_HEADS, HEAD_DIM)
    return h, k_ctx.reshape(new_shape), v_ctx.reshape(new_shape)


def kernel(x_prompt, x_sample, cache_k_l0, cache_v_l0, cache_k_l3, cache_v_l3, c, c_ctx, l0_w_ada, l0_b_ada, l0_g_mix, l0_w_qkv, l0_w_o, l0_rpb, l0_g_mlp, l0_w_up, l0_w_down, l1_w_ada, l1_b_ada, l1_g_mix, l1_w_pool, l1_b_pool, l1_pool_scale, l1_g_mlp, l1_w_up, l1_w_down, l2_w_ada, l2_b_ada, l2_g_mix, l2_w_pw1, l2_b_pw1, l2_w_dw, l2_b_dw, l2_ln_g, l2_ln_b, l2_w_pw2, l2_b_pw2, l2_g_mlp, l2_w_up, l2_w_down, l3_w_ada, l3_b_ada, l3_g_mix, l3_w_qkv, l3_w_o, l3_rpb, l3_g_mlp, l3_w_up, l3_w_down, g_final):
    h = jnp.concatenate([x_prompt.reshape(T_CTX, D), x_sample.reshape(T_LAT, D)], axis=0)

    modg = _group_modulation(c, c_ctx, l0_w_ada, l0_b_ada)
    h, new_k0, new_v0 = _attention_layer(h, modg, l0_g_mix, l0_w_qkv, l0_w_o, l0_rpb, cache_k_l0, cache_v_l0)
    h = _mlp(h, l0_g_mlp, modg, l0_w_up.astype(BF16), l0_w_down.astype(BF16))

    modg = _group_modulation(c, c_ctx, l1_w_ada, l1_b_ada)
    mixed = _pool_mix(h, l1_g_mix, modg)
    h = _pool_linear_res(mixed, l1_w_pool.astype(BF16), l1_b_pool, l1_pool_scale, h, modg)
    h = _mlp(h, l1_g_mlp, modg, l1_w_up.astype(BF16), l1_w_down.astype(BF16))

    modg = _group_modulation(c, c_ctx, l2_w_ada, l2_b_ada)
    u = _pw1_glu(h, l2_g_mix, modg, l2_w_pw1.astype(BF16), l2_b_pw1)
    u = _dwconv(u, l2_w_dw, l2_b_dw)
    h = _ln_linear_res(u, l2_ln_g, l2_ln_b, l2_w_pw2.astype(BF16), l2_b_pw2, h, modg)
    h = _mlp(h, l2_g_mlp, modg, l2_w_up.astype(BF16), l2_w_down.astype(BF16))

    modg = _group_modulation(c, c_ctx, l3_w_ada, l3_b_ada)
    h, new_k3, new_v3 = _attention_layer(h, modg, l3_g_mix, l3_w_qkv, l3_w_o, l3_rpb, cache_k_l3, cache_v_l3)
    h = _mlp(h, l3_g_mlp, modg, l3_w_up.astype(BF16), l3_w_down.astype(BF16))

    y_prompt = _final_norm(h, g_final, 0, T_CTX).reshape(x_prompt.shape)
    y_sample = _final_norm(h, g_final, T_CTX, T_LAT).reshape(x_sample.shape)
    return (y_prompt, y_sample, new_k0, new_v0, new_k3, new_v3)
```

```python
import functools
from typing import NamedTuple

import numpy as np
import jax
import jax.numpy as jnp
from jax import lax
from jax.experimental import pallas as pl
from jax.experimental.pallas import tpu as pltpu

D = 4096
N_HEADS = 32
HEAD_DIM = 128
FFN = 4 * D
GRID_W = 64
PAST = 512
WIN_H = 8
WIN_W = 16
CONV_W = 31
POOL_WINDOWS = (2, 4, 8, 16)
POOL_GROUP = D // len(POOL_WINDOWS)
N_MOD = 6
EPS = 1e-6
GROUP = 1024
MOD_ROWS = 8
N_COND = 16
MASKED = -1e30


class Stream(NamedTuple):
    name: str
    tokens: int
    seq_len: int
    mod_first: int
    mod_per_group: bool


CTX = Stream("ctx", 8192, 256, 0, False)
LAT = Stream("lat", 8192, 1024, 1, True)
GRID_H = LAT.seq_len // GRID_W

LANES = 128
SUBLANES = 8
BF16_ROWS = 16
ROW_LOOP_UNROLL = 4
VMEM_REQUEST_CAP_BYTES = 62 * 2**20
VMEM_COMPILER_MARGIN_BYTES = 12 * 2**20

F32 = jnp.float32
BF16 = jnp.bfloat16


def _nbytes(shape, dtype):
    return int(np.prod(shape)) * jnp.dtype(dtype).itemsize


def _params(double_buffered_bytes, resident_bytes, semantics):
    need = 2 * double_buffered_bytes + resident_bytes + VMEM_COMPILER_MARGIN_BYTES
    return pltpu.CompilerParams(
        dimension_semantics=semantics,
        vmem_limit_bytes=int(min(max(need, 16 * 2**20), VMEM_REQUEST_CAP_BYTES)))


def _spec(shape, index_map, single=False):
    if single:
        return pl.BlockSpec(shape, index_map, pipeline_mode=pl.Buffered(1))
    return pl.BlockSpec(shape, index_map)


def _mod_spec(stream, tm, cols=D, col_block=False):
    def index(i, j=0):
        row = stream.mod_first + ((i * tm) // GROUP if stream.mod_per_group else 0)
        return (row, 0, j if col_block else 0)
    return pl.BlockSpec((None, MOD_ROWS, cols), index)


def _rows(v, n):
    return jnp.broadcast_to(v, (n, v.shape[-1]))


def _ada_kernel(c_ref, w_ref, b_ref, o_ref):
    c = c_ref[...]
    s = (c * jax.nn.sigmoid(c)).astype(BF16)
    o_ref[...] = jnp.dot(s, w_ref[...].astype(BF16), preferred_element_type=F32) + b_ref[...]


def _modulation_table(cond, w_ada, b_ada):
    tn = 512
    n = N_MOD * D
    blocks = _nbytes((D, tn), F32) + _nbytes((1, tn), F32) + _nbytes((N_COND, tn), F32)
    resident = _nbytes((N_COND, D), F32) + _nbytes((D, tn), BF16)
    m = pl.pallas_call(
        _ada_kernel,
        grid=(n // tn,),
        in_specs=[pl.BlockSpec((N_COND, D), lambda j: (0, 0)),
                  pl.BlockSpec((D, tn), lambda j: (0, j)),
                  pl.BlockSpec((1, tn), lambda j: (0, j))],
        out_specs=pl.BlockSpec((N_COND, tn), lambda j: (0, j)),
        out_shape=jax.ShapeDtypeStruct((N_COND, n), F32),
        compiler_params=_params(blocks, resident, ("arbitrary",)),
        name="adaln",
    )(cond, w_ada, b_ada.reshape(1, n))
    m = m.reshape(N_COND, N_MOD, D)
    return jnp.concatenate([m, jnp.zeros((N_COND, MOD_ROWS - N_MOD, D), F32)], axis=1)


def _row_loop(n_rows, body):
    rc = BF16_ROWS

    def step(r, carry):
        body(pl.ds(pl.multiple_of(r * rc, rc), rc))
        return carry

    lax.fori_loop(0, n_rows // rc, step, 0, unroll=ROW_LOOP_UNROLL)


def _rms_mod_prologue(x_ref, g_ref, mod_ref, shift_row, pb_ref, xn_ref):
    pb_ref[0] = _rows(g_ref[...], BF16_ROWS)
    pb_ref[1] = _rows(1.0 + mod_ref[shift_row + 1:shift_row + 2, :], BF16_ROWS)
    pb_ref[2] = _rows(mod_ref[shift_row:shift_row + 1, :], BF16_ROWS)

    def body(rows):
        x = x_ref[rows, :]
        inv = lax.rsqrt(jnp.mean(x * x, axis=-1, keepdims=True) + EPS)
        xn_ref[rows, :] = (((x * inv) * pb_ref[0]) * pb_ref[1] + pb_ref[2]).astype(xn_ref.dtype)

    _row_loop(x_ref.shape[0], body)


def _ln_silu_prologue(x_ref, g_ref, b_ref, pb_ref, xn_ref):
    pb_ref[0] = _rows(g_ref[...], BF16_ROWS)
    pb_ref[1] = _rows(b_ref[...], BF16_ROWS)

    def body(rows):
        x = x_ref[rows, :]
        xc = x - jnp.mean(x, axis=-1, keepdims=True)
        inv = lax.rsqrt(jnp.mean(xc * xc, axis=-1, keepdims=True) + EPS)
        un = (xc * inv) * pb_ref[0] + pb_ref[1]
        xn_ref[rows, :] = (un * jax.nn.sigmoid(un)).astype(xn_ref.dtype)

    _row_loop(x_ref.shape[0], body)


def _qkv_ctx_kernel(x_ref, g_ref, mod_ref, w_ref, q_ref, k_ref, v_ref, xn_ref, pb_ref, *, n_each):
    j = pl.program_id(1)

    @pl.when(j == 0)
    def _():
        _rms_mod_prologue(x_ref, g_ref, mod_ref, 0, pb_ref, xn_ref)

    def proj():
        return jnp.dot(xn_ref[...], w_ref[...], preferred_element_type=F32)

    @pl.when(j < n_each)
    def _():
        q_ref[...] = proj().astype(q_ref.dtype)

    @pl.when(jnp.logical_and(j >= n_each, j < 2 * n_each))
    def _():
        k_ref[...] = proj()

    @pl.when(j >= 2 * n_each)
    def _():
        v_ref[...] = proj()


def _qkv_ctx(x, g, modt, w_qkv):
    tm, tn = 512, 1024
    n_each = D // tn
    t = CTX.tokens

    def park(lo):
        return lambda i, j: (i, jnp.clip(j - lo, 0, n_each - 1))

    blocks = (_nbytes((tm, D), F32) + _nbytes((1, D), F32) + _nbytes((MOD_ROWS, D), F32) + _nbytes((D, tn), BF16)
              + _nbytes((tm, tn), BF16) + 2 * _nbytes((tm, tn), F32))
    resident = _nbytes((tm, D), BF16) + _nbytes((3, BF16_ROWS, D), F32)
    return pl.pallas_call(
        functools.partial(_qkv_ctx_kernel, n_each=n_each),
        grid=(t // tm, 3 * n_each),
        in_specs=[pl.BlockSpec((tm, D), lambda i, j: (i, 0)),
                  pl.BlockSpec((1, D), lambda i, j: (0, 0)),
                  _mod_spec(CTX, tm),
                  pl.BlockSpec((D, tn), lambda i, j: (0, j))],
        out_specs=[pl.BlockSpec((tm, tn), park(0)),
                   pl.BlockSpec((tm, tn), park(n_each)),
                   pl.BlockSpec((tm, tn), park(2 * n_each))],
        out_shape=[jax.ShapeDtypeStruct((t, D), BF16),
                   jax.ShapeDtypeStruct((t, D), F32),
                   jax.ShapeDtypeStruct((t, D), F32)],
        scratch_shapes=[pltpu.VMEM((tm, D), BF16), pltpu.VMEM((3, BF16_ROWS, D), F32)],
        compiler_params=_params(blocks, resident, ("arbitrary", "arbitrary")),
        name="qkv_ctx",
    )(x, g.reshape(1, D), modt, w_qkv)


def _qkv_lat_kernel(x_ref, g_ref, mod_ref, w_ref, o_ref, xn_ref, pb_ref):
    @pl.when(pl.program_id(1) == 0)
    def _():
        _rms_mod_prologue(x_ref, g_ref, mod_ref, 0, pb_ref, xn_ref)

    y = jnp.dot(xn_ref[...], w_ref[...], preferred_element_type=F32)
    for hd in range(o_ref.shape[0]):
        o_ref[hd] = y[:, hd * HEAD_DIM:(hd + 1) * HEAD_DIM].astype(o_ref.dtype)


def _qkv_lat(x, g, modt, w_qkv):
    tm, tn = 512, 1024
    n_each = D // tn
    hpb = tn // HEAD_DIM
    t = LAT.tokens
    blocks = (_nbytes((tm, D), F32) + _nbytes((1, D), F32) + _nbytes((MOD_ROWS, D), F32) + _nbytes((D, tn), BF16)
              + _nbytes((tm, tn), BF16))
    resident = _nbytes((tm, D), BF16) + _nbytes((3, BF16_ROWS, D), F32)
    return pl.pallas_call(
        _qkv_lat_kernel,
        grid=(t // tm, 3 * n_each),
        in_specs=[pl.BlockSpec((tm, D), lambda i, j: (i, 0)),
                  pl.BlockSpec((1, D), lambda i, j: (0, 0)),
                  _mod_spec(LAT, tm),
                  pl.BlockSpec((D, tn), lambda i, j: (0, j))],
        out_specs=pl.BlockSpec((None, hpb, tm, HEAD_DIM), lambda i, j: (j // n_each, j % n_each, i, 0)),
        out_shape=jax.ShapeDtypeStruct((3, N_HEADS, t, HEAD_DIM), BF16),
        scratch_shapes=[pltpu.VMEM((tm, D), BF16), pltpu.VMEM((3, BF16_ROWS, D), F32)],
        compiler_params=_params(blocks, resident, ("arbitrary", "arbitrary")),
        name="qkv_lat",
    )(x, g.reshape(1, D), modt, w_qkv)


def _glu_kernel(x_ref, g_ref, mod_ref, wa_ref, wb_ref, ba_ref, bb_ref, o_ref, xn_ref, pb_ref):
    @pl.when(pl.program_id(1) == 0)
    def _():
        _rms_mod_prologue(x_ref, g_ref, mod_ref, 0, pb_ref, xn_ref)

    a = jnp.dot(xn_ref[...], wa_ref[...], preferred_element_type=F32) + ba_ref[...]
    b = jnp.dot(xn_ref[...], wb_ref[...], preferred_element_type=F32) + bb_ref[...]
    o_ref[...] = a * jax.nn.sigmoid(b)


def _pw1_glu(stream, x, g, modt, w_pw1, b_pw1, tm, single_x):
    tn = 512
    nj = D // tn
    x_bytes = _nbytes((tm, D), F32)
    blocks = (_nbytes((1, D), F32) + _nbytes((MOD_ROWS, D), F32) + 2 * _nbytes((D, tn), BF16)
              + 2 * _nbytes((1, tn), F32) + _nbytes((tm, tn), F32) + (0 if single_x else x_bytes))
    resident = _nbytes((tm, D), BF16) + _nbytes((3, BF16_ROWS, D), F32) + (x_bytes if single_x else 0)
    b2 = b_pw1.reshape(1, 2 * D)
    return pl.pallas_call(
        _glu_kernel,
        grid=(stream.tokens // tm, nj),
        in_specs=[_spec((tm, D), lambda i, j: (i, 0), single_x),
                  pl.BlockSpec((1, D), lambda i, j: (0, 0)),
                  _mod_spec(stream, tm),
                  pl.BlockSpec((D, tn), lambda i, j: (0, j)),
                  pl.BlockSpec((D, tn), lambda i, j: (0, nj + j)),
                  pl.BlockSpec((1, tn), lambda i, j: (0, j)),
                  pl.BlockSpec((1, tn), lambda i, j: (0, nj + j))],
        out_specs=pl.BlockSpec((tm, tn), lambda i, j: (i, j)),
        out_shape=jax.ShapeDtypeStruct((stream.tokens, D), F32),
        scratch_shapes=[pltpu.VMEM((tm, D), BF16), pltpu.VMEM((3, BF16_ROWS, D), F32)],
        compiler_params=_params(blocks, resident, ("arbitrary", "arbitrary")),
        name="pw1_glu_" + stream.name,
    )(x, g.reshape(1, D), modt, w_pw1, w_pw1, b2, b2)


def _linear_res_kernel(x_ref, w_ref, res_ref, mod_ref, o_ref):
    y = jnp.dot(x_ref[...], w_ref[...], preferred_element_type=F32)
    o_ref[...] = res_ref[...] + mod_ref[2:3, :] * y


def _linear_res(stream, x, w, res, modt, tn, tag):
    tm = 1024
    k, n = w.shape
    blocks = (_nbytes((tm, k), BF16) + _nbytes((k, tn), BF16) + 2 * _nbytes((tm, tn), F32)
              + _nbytes((MOD_ROWS, tn), F32))
    return pl.pallas_call(
        _linear_res_kernel,
        grid=(stream.tokens // tm, n // tn),
        in_specs=[pl.BlockSpec((tm, k), lambda i, j: (i, 0)),
                  pl.BlockSpec((k, tn), lambda i, j: (0, j)),
                  pl.BlockSpec((tm, tn), lambda i, j: (i, j)),
                  _mod_spec(stream, tm, tn, col_block=True)],
        out_specs=pl.BlockSpec((tm, tn), lambda i, j: (i, j)),
        out_shape=jax.ShapeDtypeStruct((stream.tokens, n), F32),
        compiler_params=_params(blocks, 0, ("arbitrary", "arbitrary")),
        name="wo_%s_%s" % (tag, stream.name),
    )(x, w, res, modt)


def _pool_linear_res_kernel(x_ref, w_ref, b_ref, s_ref, res_ref, mod_ref, o_ref):
    y = (jnp.dot(x_ref[...], w_ref[...], preferred_element_type=F32) + b_ref[...]) * s_ref[...]
    o_ref[...] = res_ref[...] + mod_ref[2:3, :] * y


def _pool_linear_res(stream, mixed, w_pool, b_pool, pool_scale, res, modt):
    tm, pg = 1024, POOL_GROUP
    blocks = (_nbytes((tm, pg), BF16) + _nbytes((pg, pg), BF16) + 2 * _nbytes((1, pg), F32)
              + 2 * _nbytes((tm, pg), F32) + _nbytes((MOD_ROWS, pg), F32))
    return pl.pallas_call(
        _pool_linear_res_kernel,
        grid=(stream.tokens // tm, len(POOL_WINDOWS)),
        in_specs=[pl.BlockSpec((tm, pg), lambda i, j: (i, j)),
                  pl.BlockSpec((None, pg, pg), lambda i, j: (j, 0, 0)),
                  pl.BlockSpec((1, pg), lambda i, j: (0, j)),
                  pl.BlockSpec((1, pg), lambda i, j: (0, j)),
                  pl.BlockSpec((tm, pg), lambda i, j: (i, j)),
                  _mod_spec(stream, tm, pg, col_block=True)],
        out_specs=pl.BlockSpec((tm, pg), lambda i, j: (i, j)),
        out_shape=jax.ShapeDtypeStruct((stream.tokens, D), F32),
        compiler_params=_params(blocks, 0, ("arbitrary", "arbitrary")),
        name="pool_linear_" + stream.name,
    )(mixed, w_pool, b_pool.reshape(1, D), pool_scale.reshape(1, D), res, modt)


def _ln_linear_res_kernel(x_ref, lg_ref, lb_ref, w_ref, b_ref, res_ref, mod_ref, o_ref, xn_ref, pb_ref):
    @pl.when(pl.program_id(1) == 0)
    def _():
        _ln_silu_prologue(x_ref, lg_ref, lb_ref, pb_ref, xn_ref)

    y = jnp.dot(xn_ref[...], w_ref[...], preferred_element_type=F32) + b_ref[...]
    o_ref[...] = res_ref[...] + mod_ref[2:3, :] * y


def _ln_linear_res(stream, u, ln_g, ln_b, w, b, res, modt, tm, tn, single_x):
    x_bytes = _nbytes((tm, D), F32)
    blocks = (2 * _nbytes((1, D), F32) + _nbytes((D, tn), BF16) + _nbytes((1, tn), F32)
              + 2 * _nbytes((tm, tn), F32) + _nbytes((MOD_ROWS, tn), F32) + (0 if single_x else x_bytes))
    resident = _nbytes((tm, D), BF16) + _nbytes((3, BF16_ROWS, D), F32) + (x_bytes if single_x else 0)
    return pl.pallas_call(
        _ln_linear_res_kernel,
        grid=(stream.tokens // tm, D // tn),
        in_specs=[_spec((tm, D), lambda i, j: (i, 0), single_x),
                  pl.BlockSpec((1, D), lambda i, j: (0, 0)),
                  pl.BlockSpec((1, D), lambda i, j: (0, 0)),
                  pl.BlockSpec((D, tn), lambda i, j: (0, j)),
                  pl.BlockSpec((1, tn), lambda i, j: (0, j)),
                  pl.BlockSpec((tm, tn), lambda i, j: (i, j)),
                  _mod_spec(stream, tm, tn, col_block=True)],
        out_specs=pl.BlockSpec((tm, tn), lambda i, j: (i, j)),
        out_shape=jax.ShapeDtypeStruct((stream.tokens, D), F32),
        scratch_shapes=[pltpu.VMEM((tm, D), BF16), pltpu.VMEM((3, BF16_ROWS, D), F32)],
        compiler_params=_params(blocks, resident, ("arbitrary", "arbitrary")),
        name="pw2_" + stream.name,
    )(u, ln_g.reshape(1, D), ln_b.reshape(1, D), w, b.reshape(1, D), res, modt)


def _mlp_kernel(*refs, final_norm):
    if final_norm:
        x_ref, g_ref, mod_ref, wup_ref, wdn_ref, gf_ref, o_ref, xn_ref, pb_ref = refs
    else:
        x_ref, g_ref, mod_ref, wup_ref, wdn_ref, o_ref, xn_ref, pb_ref = refs
    f = pl.program_id(1)

    @pl.when(f == 0)
    def _():
        _rms_mod_prologue(x_ref, g_ref, mod_ref, 3, pb_ref, xn_ref)
        o_ref[...] = jnp.zeros_like(o_ref)

    up = jnp.dot(xn_ref[...], wup_ref[...], preferred_element_type=F32)
    act = jnp.square(jnp.maximum(up, 0.0)).astype(BF16)
    o_ref[...] += jnp.dot(act, wdn_ref[...], preferred_element_type=F32)

    @pl.when(f == pl.num_programs(1) - 1)
    def _():
        pb_ref[0] = _rows(mod_ref[5:6, :], BF16_ROWS)
        if final_norm:
            pb_ref[1] = _rows(gf_ref[...], BF16_ROWS)

        def body(rows):
            y = x_ref[rows, :] + pb_ref[0] * o_ref[rows, :]
            if final_norm:
                y = (y * lax.rsqrt(jnp.mean(y * y, axis=-1, keepdims=True) + EPS)) * pb_ref[1]
            o_ref[rows, :] = y

        _row_loop(x_ref.shape[0], body)


def _mlp(stream, x, g, modt, w_up, w_down, tm, tf, single, tag, g_final=None):
    xo_bytes = 2 * _nbytes((tm, D), F32)
    blocks = (_nbytes((1, D), F32) + _nbytes((MOD_ROWS, D), F32) + _nbytes((D, tf), BF16) + _nbytes((tf, D), BF16)
              + (0 if single else xo_bytes))
    resident = _nbytes((tm, D), BF16) + _nbytes((3, BF16_ROWS, D), F32) + (xo_bytes if single else 0)
    in_specs = [_spec((tm, D), lambda i, f: (i, 0), single),
                pl.BlockSpec((1, D), lambda i, f: (0, 0)),
                _mod_spec(stream, tm),
                pl.BlockSpec((D, tf), lambda i, f: (0, f)),
                pl.BlockSpec((tf, D), lambda i, f: (f, 0))]
    args = [x, g.reshape(1, D), modt, w_up, w_down]
    if g_final is not None:
        in_specs.append(pl.BlockSpec((1, D), lambda i, f: (0, 0)))
        args.append(g_final.reshape(1, D))
        blocks += _nbytes((1, D), F32)
    return pl.pallas_call(
        functools.partial(_mlp_kernel, final_norm=g_final is not None),
        grid=(stream.tokens // tm, FFN // tf),
        in_specs=in_specs,
        out_specs=_spec((tm, D), lambda i, f: (i, 0), single),
        out_shape=jax.ShapeDtypeStruct((stream.tokens, D), F32),
        scratch_shapes=[pltpu.VMEM((tm, D), BF16), pltpu.VMEM((3, BF16_ROWS, D), F32)],
        compiler_params=_params(blocks, resident, ("arbitrary", "arbitrary")),
        name="mlp_%s_%s" % (tag, stream.name),
    )(*args)


def _dot_nt(a, b):
    return lax.dot_general(a, b, (((1,), (1,)), ((), ())), preferred_element_type=F32)


def _ctx_attn_kernel(q_ref, k_ref, v_ref, o_ref):
    scale = HEAD_DIM ** -0.5
    for hd in range(N_HEADS):
        cols = slice(hd * HEAD_DIM, (hd + 1) * HEAD_DIM)
        q = q_ref[:, cols]
        k = k_ref[:, cols].astype(BF16)
        v = v_ref[:, cols].astype(BF16)
        s = _dot_nt(q, k) * scale
        p = jnp.exp(s - jnp.max(s, axis=-1, keepdims=True))
        denom = jnp.sum(p, axis=-1, keepdims=True)
        o = jnp.dot(p.astype(BF16), v, preferred_element_type=F32) / denom
        o_ref[:, cols] = o.astype(o_ref.dtype)


def _ctx_attention(q, k, v):
    seq = CTX.seq_len
    blocks = 2 * _nbytes((seq, D), BF16) + 2 * _nbytes((seq, D), F32)
    return pl.pallas_call(
        _ctx_attn_kernel,
        grid=(CTX.tokens // seq,),
        in_specs=[pl.BlockSpec((seq, D), lambda b: (b, 0))] * 3,
        out_specs=pl.BlockSpec((seq, D), lambda b: (b, 0)),
        out_shape=jax.ShapeDtypeStruct((CTX.tokens, D), BF16),
        compiler_params=_params(blocks, 0, ("arbitrary",)),
        name="ctx_attention",
    )(q, k, v)


Q_CHUNK_ROWS = 4
CACHE_HEADS = 8


def _cache_heads_kernel(c_ref, o_ref):
    for hd in range(CACHE_HEADS):
        o_ref[hd] = c_ref[:, hd, :].astype(o_ref.dtype)


def _cache_heads(cache):
    nb = cache.shape[0]
    blocks = _nbytes((PAST, CACHE_HEADS, HEAD_DIM), F32) + _nbytes((CACHE_HEADS, PAST, HEAD_DIM), BF16)
    return pl.pallas_call(
        _cache_heads_kernel,
        grid=(nb, N_HEADS // CACHE_HEADS),
        in_specs=[pl.BlockSpec((None, PAST, CACHE_HEADS, HEAD_DIM), lambda b, g: (b, 0, g, 0))],
        out_specs=pl.BlockSpec((None, CACHE_HEADS, PAST, HEAD_DIM), lambda b, g: (b, g, 0, 0)),
        out_shape=jax.ShapeDtypeStruct((nb, N_HEADS, PAST, HEAD_DIM), BF16),
        compiler_params=_params(blocks, 0, ("arbitrary", "arbitrary")),
        name="cache_heads",
    )(cache)


def _row_start(r):
    return min(max(r - WIN_H // 2, 0), GRID_H - WIN_H)


def _chunk_key_rows(c):
    r0, r1 = c * Q_CHUNK_ROWS, (c + 1) * Q_CHUNK_ROWS - 1
    lo, hi = _row_start(r0), _row_start(r1) + WIN_H
    lo -= lo % 2
    hi += hi % 2
    return lo, hi


def _natten_kernel(q_ref, k_ref, v_ref, kc_ref, vc_ref, tab_ref, o_ref, bias_ref):
    scale = HEAD_DIM ** -0.5

    @pl.when(pl.program_id(1) == 0)
    def _():
        left = lax.broadcasted_iota(jnp.int32, (GRID_W, LANES), 1) < GRID_W
        masked = jnp.full((GRID_W, LANES), MASKED, F32)
        for r in range(GRID_H):
            rs = _row_start(r)
            for pair in range(GRID_H // 2):
                halves = []
                for kr in (2 * pair, 2 * pair + 1):
                    halves.append(tab_ref[kr - r + WIN_H - 1] if rs <= kr < rs + WIN_H else masked)
                bias_ref[r * GRID_W:(r + 1) * GRID_W, pair * LANES:(pair + 1) * LANES] = (
                    jnp.where(left, halves[0], halves[1]))

    kc = kc_ref[...]
    vc = vc_ref[...]
    nq = Q_CHUNK_ROWS * GRID_W
    for c in range(GRID_H // Q_CHUNK_ROWS):
        lo, hi = _chunk_key_rows(c)
        keys = slice(lo * GRID_W, hi * GRID_W)
        q = q_ref[c * nq:(c + 1) * nq, :]
        s_loc = _dot_nt(q, k_ref[keys, :]) * scale + bias_ref[c * nq:(c + 1) * nq, keys]
        s_ctx = _dot_nt(q, kc) * scale
        m = jnp.maximum(jnp.max(s_loc, axis=-1, keepdims=True), jnp.max(s_ctx, axis=-1, keepdims=True))
        p_loc = jnp.exp(s_loc - m)
        p_ctx = jnp.exp(s_ctx - m)
        denom = jnp.sum(p_loc, axis=-1, keepdims=True) + jnp.sum(p_ctx, axis=-1, keepdims=True)
        o = (jnp.dot(p_loc.astype(BF16), v_ref[keys, :], preferred_element_type=F32)
             + jnp.dot(p_ctx.astype(BF16), vc, preferred_element_type=F32)) / denom
        o_ref[c * nq:(c + 1) * nq, :] = o.astype(o_ref.dtype)


def _column_bias_tables(rpb):
    cols = np.arange(GRID_W)
    start = np.clip(cols - WIN_W // 2, 0, GRID_W - WIN_W)
    kc = cols[None, :]
    valid = (kc >= start[:, None]) & (kc < start[:, None] + WIN_W)
    dc = np.clip(kc - cols[:, None] + WIN_W - 1, 0, 2 * WIN_W - 2)
    tab = jnp.where(valid[None, None], rpb[:, :, dc], MASKED)
    return jnp.concatenate([tab, tab], axis=-1)


def _natten(qkv, cache_k, cache_v, rpb):
    tab = _column_bias_tables(rpb)
    n_dr = 2 * WIN_H - 1
    seq = LAT.seq_len
    blocks = (4 * _nbytes((seq, HEAD_DIM), BF16) + 2 * _nbytes((PAST, HEAD_DIM), BF16)
              + _nbytes((n_dr, GRID_W, LANES), F32))
    resident = _nbytes((seq, seq), F32)

    def qkv_spec(which):
        return pl.BlockSpec((None, None, seq, HEAD_DIM), lambda hd, b: (which, hd, b, 0))

    cache_spec = pl.BlockSpec((None, None, PAST, HEAD_DIM), lambda hd, b: (b, hd, 0, 0))
    return pl.pallas_call(
        _natten_kernel,
        grid=(N_HEADS, LAT.tokens // seq),
        in_specs=[qkv_spec(0), qkv_spec(1), qkv_spec(2), cache_spec, cache_spec,
                  pl.BlockSpec((None, n_dr, GRID_W, LANES), lambda hd, b: (hd, 0, 0, 0))],
        out_specs=pl.BlockSpec((seq, HEAD_DIM), lambda hd, b: (b, hd)),
        out_shape=jax.ShapeDtypeStruct((LAT.tokens, D), BF16),
        scratch_shapes=[pltpu.VMEM((seq, seq), F32)],
        compiler_params=_params(blocks, resident, ("arbitrary", "arbitrary")),
        name="natten",
    )(qkv, qkv, qkv, cache_k, cache_v, tab)


def _shift_rows(y, s, pos, seq_len):
    rolled = pltpu.roll(y, s % y.shape[0], axis=0)
    src = pos - s
    return jnp.where(jnp.logical_and(src >= 0, src < seq_len), rolled, 0.0)


def _window_sum(x, half, pos, seq_len):
    back = _shift_rows(x, 1, pos, seq_len)
    fwd = x
    n = 1
    while n < half:
        back = back + _shift_rows(back, n, pos, seq_len)
        fwd = fwd + _shift_rows(fwd, -n, pos, seq_len)
        n *= 2
    return back + fwd


def _pool_mix_kernel(x_ref, g_ref, mod_ref, o_ref, inv_ref, *, seq_len):
    def stats(rows):
        x = x_ref[rows, :]
        inv = lax.rsqrt(jnp.mean(x * x, axis=-1, keepdims=True) + EPS)
        inv_ref[rows, :] = jnp.broadcast_to(inv, (BF16_ROWS, LANES))

    _row_loop(GROUP, stats)

    for grp, window in enumerate(POOL_WINDOWS):
        half = window // 2

        def block(blk, carry, grp=grp, half=half):
            cols = pl.ds(pl.multiple_of(grp * POOL_GROUP + blk * LANES, LANES), LANES)
            pos = lax.broadcasted_iota(jnp.int32, (GROUP, LANES), 0) & (seq_len - 1)
            cnt = (jnp.minimum(pos + half, seq_len) - jnp.maximum(pos - half, 0)).astype(F32)
            up = ((x_ref[:, cols] * inv_ref[...]) * g_ref[:, cols]) * (1.0 + mod_ref[1:2, cols]) + mod_ref[0:1, cols]
            pooled = _window_sum(up, half, pos, seq_len) / cnt
            o_ref[:, cols] = (pooled - up).astype(o_ref.dtype)
            return carry

        lax.fori_loop(0, POOL_GROUP // LANES, block, 0)


def _pool_mix(stream, x, g, modt):
    blocks = _nbytes((GROUP, D), F32) + _nbytes((1, D), F32) + _nbytes((MOD_ROWS, D), F32) + _nbytes((GROUP, D), BF16)
    resident = _nbytes((GROUP, LANES), F32)
    return pl.pallas_call(
        functools.partial(_pool_mix_kernel, seq_len=stream.seq_len),
        grid=(stream.tokens // GROUP,),
        in_specs=[pl.BlockSpec((GROUP, D), lambda i: (i, 0)),
                  pl.BlockSpec((1, D), lambda i: (0, 0)),
                  _mod_spec(stream, GROUP)],
        out_specs=pl.BlockSpec((GROUP, D), lambda i: (i, 0)),
        out_shape=jax.ShapeDtypeStruct((stream.tokens, D), BF16),
        scratch_shapes=[pltpu.VMEM((GROUP, LANES), F32)],
        compiler_params=_params(blocks, resident, ("arbitrary",)),
        name="pool_mix_" + stream.name,
    )(x, g.reshape(1, D), modt)


CONV_HALO = 16
CONV_BLOCK = 512
CONV_ROWS = 64


def _dwconv_kernel(u_ref, w_ref, b_ref, o_ref, pad_ref, *, seq_len):
    n_seq = GROUP // seq_len
    stride = seq_len + CONV_HALO
    zeros = jnp.zeros((CONV_HALO, LANES), F32)
    for lb in range(CONV_BLOCK // LANES):
        cols = slice(lb * LANES, (lb + 1) * LANES)
        for s in range(n_seq + 1):
            pad_ref[lb, s * stride:s * stride + CONV_HALO, :] = zeros
        for s in range(n_seq):
            top = CONV_HALO + s * stride
            pad_ref[lb, top:top + seq_len, :] = u_ref[s * seq_len:(s + 1) * seq_len, cols]
        bias = jnp.broadcast_to(b_ref[:, cols], (SUBLANES, LANES))[None]
        for s in range(n_seq):
            for ch in range(seq_len // CONV_ROWS):
                first = CONV_HALO + s * stride + ch * CONV_ROWS - CONV_W // 2
                acc = jnp.broadcast_to(bias, (CONV_ROWS // SUBLANES, SUBLANES, LANES))
                for tap in range(CONV_W):
                    x = pad_ref[lb, first + tap:first + tap + CONV_ROWS, :]
                    acc = acc + x.reshape(CONV_ROWS // SUBLANES, SUBLANES, LANES) * w_ref[tap, :, cols][None]
                out_row = s * seq_len + ch * CONV_ROWS
                o_ref[out_row:out_row + CONV_ROWS, cols] = acc.reshape(CONV_ROWS, LANES)


def _dwconv(stream, u, w_dw, b_dw):
    w8 = jnp.broadcast_to(w_dw[:, None, :], (CONV_W, SUBLANES, D))
    n_seq = GROUP // stream.seq_len
    pad_rows = CONV_HALO + n_seq * (stream.seq_len + CONV_HALO)
    blocks = 2 * _nbytes((GROUP, CONV_BLOCK), F32) + _nbytes((CONV_W, SUBLANES, CONV_BLOCK), F32)
    resident = _nbytes((CONV_BLOCK // LANES, pad_rows, LANES), F32)
    return pl.pallas_call(
        functools.partial(_dwconv_kernel, seq_len=stream.seq_len),
        grid=(stream.tokens // GROUP, D // CONV_BLOCK),
        in_specs=[pl.BlockSpec((GROUP, CONV_BLOCK), lambda i, j: (i, j)),
                  pl.BlockSpec((CONV_W, SUBLANES, CONV_BLOCK), lambda i, j: (0, 0, j)),
                  pl.BlockSpec((1, CONV_BLOCK), lambda i, j: (0, j))],
        out_specs=pl.BlockSpec((GROUP, CONV_BLOCK), lambda i, j: (i, j)),
        out_shape=jax.ShapeDtypeStruct((stream.tokens, D), F32),
        scratch_shapes=[pltpu.VMEM((CONV_BLOCK // LANES, pad_rows, LANES), F32)],
        compiler_params=_params(blocks, resident, ("arbitrary", "arbitrary")),
        name="dwconv_" + stream.name,
    )(u, w8, b_dw.reshape(1, D))


def _attention_layer(hp, hs, modt, g_mix, w_qkv, w_o, rpb, cache_k, cache_v, tag, wo_tn):
    w_qkv = w_qkv.astype(BF16)
    w_o = w_o.astype(BF16)
    q_ctx, k_ctx, v_ctx = _qkv_ctx(hp, g_mix, modt, w_qkv)
    o_ctx = _ctx_attention(q_ctx, k_ctx, v_ctx)
    hp = _linear_res(CTX, o_ctx, w_o, hp, modt, wo_tn[0], tag)
    qkv_lat = _qkv_lat(hs, g_mix, modt, w_qkv)
    o_lat = _natten(qkv_lat, _cache_heads(cache_k), _cache_heads(cache_v), rpb)
    hs = _linear_res(LAT, o_lat, w_o, hs, modt, wo_tn[1], tag)
    new_shape = (CTX.tokens // CTX.seq_len, CTX.seq_len, N_HEADS, HEAD_DIM)
    return hp, hs, k_ctx.reshape(new_shape), v_ctx.reshape(new_shape)


def kernel(x_prompt, x_sample, cache_k_l0, cache_v_l0, cache_k_l3, cache_v_l3, c, c_ctx, l0_w_ada, l0_b_ada, l0_g_mix, l0_w_qkv, l0_w_o, l0_rpb, l0_g_mlp, l0_w_up, l0_w_down, l1_w_ada, l1_b_ada, l1_g_mix, l1_w_pool, l1_b_pool, l1_pool_scale, l1_g_mlp, l1_w_up, l1_w_down, l2_w_ada, l2_b_ada, l2_g_mix, l2_w_pw1, l2_b_pw1, l2_w_dw, l2_b_dw, l2_ln_g, l2_ln_b, l2_w_pw2, l2_b_pw2, l2_g_mlp, l2_w_up, l2_w_down, l3_w_ada, l3_b_ada, l3_g_mix, l3_w_qkv, l3_w_o, l3_rpb, l3_g_mlp, l3_w_up, l3_w_down, g_final):
    hp = x_prompt.reshape(CTX.tokens, D)
    hs = x_sample.reshape(LAT.tokens, D)
    cond = jnp.concatenate([c_ctx[None], c, jnp.zeros((N_COND - 1 - c.shape[0], D), F32)], axis=0)

    modt = _modulation_table(cond, l0_w_ada, l0_b_ada)
    hp, hs, new_k0, new_v0 = _attention_layer(hp, hs, modt, l0_g_mix, l0_w_qkv, l0_w_o, l0_rpb,
                                              cache_k_l0, cache_v_l0, "l0", (512, 1024))
    w_up, w_down = l0_w_up.astype(BF16), l0_w_down.astype(BF16)
    hp = _mlp(CTX, hp, l0_g_mlp, modt, w_up, w_down, 512, 512, False, "l0")
    hs = _mlp(LAT, hs, l0_g_mlp, modt, w_up, w_down, 512, 512, False, "l0")

    modt = _modulation_table(cond, l1_w_ada, l1_b_ada)
    w_pool = l1_w_pool.astype(BF16)
    hp = _pool_linear_res(CTX, _pool_mix(CTX, hp, l1_g_mix, modt), w_pool, l1_b_pool, l1_pool_scale, hp, modt)
    hs = _pool_linear_res(LAT, _pool_mix(LAT, hs, l1_g_mix, modt), w_pool, l1_b_pool, l1_pool_scale, hs, modt)
    w_up, w_down = l1_w_up.astype(BF16), l1_w_down.astype(BF16)
    hp = _mlp(CTX, hp, l1_g_mlp, modt, w_up, w_down, 1024, 512, True, "l1")
    hs = _mlp(LAT, hs, l1_g_mlp, modt, w_up, w_down, 512, 512, False, "l1")

    modt = _modulation_table(cond, l2_w_ada, l2_b_ada)
    w_pw1, w_pw2 = l2_w_pw1.astype(BF16), l2_w_pw2.astype(BF16)
    up = _dwconv(CTX, _pw1_glu(CTX, hp, l2_g_mix, modt, w_pw1, l2_b_pw1, 512, False), l2_w_dw, l2_b_dw)
    hp = _ln_linear_res(CTX, up, l2_ln_g, l2_ln_b, w_pw2, l2_b_pw2, hp, modt, 512, 1024, False)
    us = _dwconv(LAT, _pw1_glu(LAT, hs, l2_g_mix, modt, w_pw1, l2_b_pw1, 1024, True), l2_w_dw, l2_b_dw)
    hs = _ln_linear_res(LAT, us, l2_ln_g, l2_ln_b, w_pw2, l2_b_pw2, hs, modt, 1024, 512, True)
    w_up, w_down = l2_w_up.astype(BF16), l2_w_down.astype(BF16)
    hp = _mlp(CTX, hp, l2_g_mlp, modt, w_up, w_down, 512, 512, False, "l2")
    hs = _mlp(LAT, hs, l2_g_mlp, modt, w_up, w_down, 512, 512, False, "l2")

    modt = _modulation_table(cond, l3_w_ada, l3_b_ada)
    hp, hs, new_k3, new_v3 = _attention_layer(hp, hs, modt, l3_g_mix, l3_w_qkv, l3_w_o, l3_rpb,
                                              cache_k_l3, cache_v_l3, "l3", (1024, 512))
    w_up, w_down = l3_w_up.astype(BF16), l3_w_down.astype(BF16)
    y_prompt = _mlp(CTX, hp, l3_g_mlp, modt, w_up, w_down, 512, 512, False, "l3", g_final)
    y_sample = _mlp(LAT, hs, l3_g_mlp, modt, w_up, w_down, 512, 512, False, "l3", g_final)
    return (y_prompt.reshape(x_prompt.shape), y_sample.reshape(x_sample.shape), new_k0, new_v0, new_k3, new_v3)
```

```python
import functools
from typing import NamedTuple

import numpy as np
import jax
import jax.numpy as jnp
from jax import lax
from jax.experimental import pallas as pl
from jax.experimental.pallas import tpu as pltpu

D = 4096
N_HEADS = 32
HEAD_DIM = 128
FFN = 4 * D
GRID_W = 64
PAST = 512
WIN_H = 8
WIN_W = 16
CONV_W = 31
POOL_WINDOWS = (2, 4, 8, 16)
POOL_GROUP = D // len(POOL_WINDOWS)
N_MOD = 6
EPS = 1e-6
GROUP = 1024
MOD_ROWS = 8
N_COND = 16
MASKED = -1e30


class Stream(NamedTuple):
    name: str
    tokens: int
    seq_len: int
    mod_first: int
    mod_per_group: bool


CTX = Stream("ctx", 8192, 256, 0, False)
LAT = Stream("lat", 8192, 1024, 1, True)
GRID_H = LAT.seq_len // GRID_W

LANES = 128
SUBLANES = 8
BF16_ROWS = 16
ROW_LOOP_UNROLL = 4
VMEM_REQUEST_CAP_BYTES = 62 * 2**20
VMEM_COMPILER_MARGIN_BYTES = 12 * 2**20

F32 = jnp.float32
BF16 = jnp.bfloat16


def _nbytes(shape, dtype):
    return int(np.prod(shape)) * jnp.dtype(dtype).itemsize


def _params(double_buffered_bytes, resident_bytes, semantics):
    need = 2 * double_buffered_bytes + resident_bytes + VMEM_COMPILER_MARGIN_BYTES
    return pltpu.CompilerParams(
        dimension_semantics=semantics,
        vmem_limit_bytes=int(min(max(need, 16 * 2**20), VMEM_REQUEST_CAP_BYTES)))


def _mod_spec(stream, tm, cols=D, col_block=False):
    def index(i, j=0):
        row = stream.mod_first + ((i * tm) // GROUP if stream.mod_per_group else 0)
        return (row, 0, j if col_block else 0)
    return pl.BlockSpec((None, MOD_ROWS, cols), index)


def _rows(v, n):
    return jnp.broadcast_to(v, (n, v.shape[-1]))


def _ada_kernel(c_ref, w_ref, b_ref, o_ref):
    c = c_ref[...]
    s = (c * jax.nn.sigmoid(c)).astype(BF16)
    o_ref[...] = jnp.dot(s, w_ref[...].astype(BF16), preferred_element_type=F32) + b_ref[...]


def _modulation_table(cond, w_ada, b_ada):
    tn = 512
    n = N_MOD * D
    blocks = _nbytes((D, tn), F32) + _nbytes((1, tn), F32) + _nbytes((N_COND, tn), F32)
    resident = _nbytes((N_COND, D), F32) + _nbytes((D, tn), BF16)
    m = pl.pallas_call(
        _ada_kernel,
        grid=(n // tn,),
        in_specs=[pl.BlockSpec((N_COND, D), lambda j: (0, 0)),
                  pl.BlockSpec((D, tn), lambda j: (0, j)),
                  pl.BlockSpec((1, tn), lambda j: (0, j))],
        out_specs=pl.BlockSpec((N_COND, tn), lambda j: (0, j)),
        out_shape=jax.ShapeDtypeStruct((N_COND, n), F32),
        compiler_params=_params(blocks, resident, ("arbitrary",)),
        name="adaln",
    )(cond, w_ada, b_ada.reshape(1, n))
    m = m.reshape(N_COND, N_MOD, D)
    return jnp.concatenate([m, jnp.zeros((N_COND, MOD_ROWS - N_MOD, D), F32)], axis=1)


def _row_loop(n_rows, body):
    rc = BF16_ROWS

    def step(r, carry):
        body(pl.ds(pl.multiple_of(r * rc, rc), rc))
        return carry

    lax.fori_loop(0, n_rows // rc, step, 0, unroll=ROW_LOOP_UNROLL)


def _rms_mod_prologue(x_ref, g_ref, mod_ref, shift_row, pb_ref, xn_ref):
    pb_ref[0] = _rows(g_ref[...], BF16_ROWS)
    pb_ref[1] = _rows(1.0 + mod_ref[shift_row + 1:shift_row + 2, :], BF16_ROWS)
    pb_ref[2] = _rows(mod_ref[shift_row:shift_row + 1, :], BF16_ROWS)

    def body(rows):
        x = x_ref[rows, :]
        inv = lax.rsqrt(jnp.mean(x * x, axis=-1, keepdims=True) + EPS)
        xn_ref[rows, :] = (((x * inv) * pb_ref[0]) * pb_ref[1] + pb_ref[2]).astype(xn_ref.dtype)

    _row_loop(x_ref.shape[0], body)


def _ln_silu_prologue(x_ref, g_ref, b_ref, pb_ref, xn_ref):
    pb_ref[0] = _rows(g_ref[...], BF16_ROWS)
    pb_ref[1] = _rows(b_ref[...], BF16_ROWS)

    def body(rows):
        x = x_ref[rows, :]
        xc = x - jnp.mean(x, axis=-1, keepdims=True)
        inv = lax.rsqrt(jnp.mean(xc * xc, axis=-1, keepdims=True) + EPS)
        un = (xc * inv) * pb_ref[0] + pb_ref[1]
        xn_ref[rows, :] = (un * jax.nn.sigmoid(un)).astype(xn_ref.dtype)

    _row_loop(x_ref.shape[0], body)


def _qkv_ctx_kernel(x_ref, g_ref, mod_ref, w_ref, q_ref, k_ref, v_ref, xn_ref, pb_ref, *, n_each):
    j = pl.program_id(1)

    @pl.when(j == 0)
    def _():
        _rms_mod_prologue(x_ref, g_ref, mod_ref, 0, pb_ref, xn_ref)

    def proj():
        return jnp.dot(xn_ref[...], w_ref[...], preferred_element_type=F32)

    @pl.when(j < n_each)
    def _():
        q_ref[...] = proj().astype(q_ref.dtype)

    @pl.when(jnp.logical_and(j >= n_each, j < 2 * n_each))
    def _():
        k_ref[...] = proj()

    @pl.when(j >= 2 * n_each)
    def _():
        v_ref[...] = proj()


def _qkv_ctx(x, g, modt, w_qkv):
    tm, tn = 512, 1024
    n_each = D // tn
    t = CTX.tokens

    def park(lo):
        return lambda i, j: (i, jnp.clip(j - lo, 0, n_each - 1))

    blocks = (_nbytes((tm, D), F32) + _nbytes((1, D), F32) + _nbytes((MOD_ROWS, D), F32) + _nbytes((D, tn), BF16)
              + _nbytes((tm, tn), BF16) + 2 * _nbytes((tm, tn), F32))
    resident = _nbytes((tm, D), BF16) + _nbytes((3, BF16_ROWS, D), F32)
    return pl.pallas_call(
        functools.partial(_qkv_ctx_kernel, n_each=n_each),
        grid=(t // tm, 3 * n_each),
        in_specs=[pl.BlockSpec((tm, D), lambda i, j: (i, 0)),
                  pl.BlockSpec((1, D), lambda i, j: (0, 0)),
                  _mod_spec(CTX, tm),
                  pl.BlockSpec((D, tn), lambda i, j: (0, j))],
        out_specs=[pl.BlockSpec((tm, tn), park(0)),
                   pl.BlockSpec((tm, tn), park(n_each)),
                   pl.BlockSpec((tm, tn), park(2 * n_each))],
        out_shape=[jax.ShapeDtypeStruct((t, D), BF16),
                   jax.ShapeDtypeStruct((t, D), F32),
                   jax.ShapeDtypeStruct((t, D), F32)],
        scratch_shapes=[pltpu.VMEM((tm, D), BF16), pltpu.VMEM((3, BF16_ROWS, D), F32)],
        compiler_params=_params(blocks, resident, ("arbitrary", "arbitrary")),
        name="qkv_ctx",
    )(x, g.reshape(1, D), modt, w_qkv)


def _qkv_lat_kernel(x_ref, g_ref, mod_ref, w_ref, o_ref, xn_ref, pb_ref):
    @pl.when(pl.program_id(1) == 0)
    def _():
        _rms_mod_prologue(x_ref, g_ref, mod_ref, 0, pb_ref, xn_ref)

    y = jnp.dot(xn_ref[...], w_ref[...], preferred_element_type=F32)
    for hd in range(o_ref.shape[0]):
        o_ref[hd] = y[:, hd * HEAD_DIM:(hd + 1) * HEAD_DIM].astype(o_ref.dtype)


def _qkv_lat(x, g, modt, w_qkv):
    tm, tn = 512, 1024
    n_each = D // tn
    hpb = tn // HEAD_DIM
    t = LAT.tokens
    blocks = (_nbytes((tm, D), F32) + _nbytes((1, D), F32) + _nbytes((MOD_ROWS, D), F32) + _nbytes((D, tn), BF16)
              + _nbytes((tm, tn), BF16))
    resident = _nbytes((tm, D), BF16) + _nbytes((3, BF16_ROWS, D), F32)
    return pl.pallas_call(
        _qkv_lat_kernel,
        grid=(t // tm, 3 * n_each),
        in_specs=[pl.BlockSpec((tm, D), lambda i, j: (i, 0)),
                  pl.BlockSpec((1, D), lambda i, j: (0, 0)),
                  _mod_spec(LAT, tm),
                  pl.BlockSpec((D, tn), lambda i, j: (0, j))],
        out_specs=pl.BlockSpec((None, hpb, tm, HEAD_DIM), lambda i, j: (j // n_each, j % n_each, i, 0)),
        out_shape=jax.ShapeDtypeStruct((3, N_HEADS, t, HEAD_DIM), BF16),
        scratch_shapes=[pltpu.VMEM((tm, D), BF16), pltpu.VMEM((3, BF16_ROWS, D), F32)],
        compiler_params=_params(blocks, resident, ("arbitrary", "arbitrary")),
        name="qkv_lat",
    )(x, g.reshape(1, D), modt, w_qkv)


def _glu_kernel(x_ref, g_ref, mod_ref, wa_ref, wb_ref, ba_ref, bb_ref, o_ref, xn_ref, pb_ref):
    @pl.when(pl.program_id(1) == 0)
    def _():
        _rms_mod_prologue(x_ref, g_ref, mod_ref, 0, pb_ref, xn_ref)

    a = jnp.dot(xn_ref[...], wa_ref[...], preferred_element_type=F32) + ba_ref[...]
    b = jnp.dot(xn_ref[...], wb_ref[...], preferred_element_type=F32) + bb_ref[...]
    o_ref[...] = a * jax.nn.sigmoid(b)


def _pw1_glu(stream, x, g, modt, w_pw1, b_pw1):
    tm, tn = 512, 512
    nj = D // tn
    blocks = (_nbytes((tm, D), F32) + _nbytes((1, D), F32) + _nbytes((MOD_ROWS, D), F32) + 2 * _nbytes((D, tn), BF16)
              + 2 * _nbytes((1, tn), F32) + _nbytes((tm, tn), F32))
    resident = _nbytes((tm, D), BF16) + _nbytes((3, BF16_ROWS, D), F32)
    b2 = b_pw1.reshape(1, 2 * D)
    return pl.pallas_call(
        _glu_kernel,
        grid=(stream.tokens // tm, nj),
        in_specs=[pl.BlockSpec((tm, D), lambda i, j: (i, 0)),
                  pl.BlockSpec((1, D), lambda i, j: (0, 0)),
                  _mod_spec(stream, tm),
                  pl.BlockSpec((D, tn), lambda i, j: (0, j)),
                  pl.BlockSpec((D, tn), lambda i, j: (0, nj + j)),
                  pl.BlockSpec((1, tn), lambda i, j: (0, j)),
                  pl.BlockSpec((1, tn), lambda i, j: (0, nj + j))],
        out_specs=pl.BlockSpec((tm, tn), lambda i, j: (i, j)),
        out_shape=jax.ShapeDtypeStruct((stream.tokens, D), F32),
        scratch_shapes=[pltpu.VMEM((tm, D), BF16), pltpu.VMEM((3, BF16_ROWS, D), F32)],
        compiler_params=_params(blocks, resident, ("arbitrary", "arbitrary")),
        name="pw1_glu_" + stream.name,
    )(x, g.reshape(1, D), modt, w_pw1, w_pw1, b2, b2)


def _linear_res_kernel(x_ref, w_ref, res_ref, mod_ref, o_ref):
    y = jnp.dot(x_ref[...], w_ref[...], preferred_element_type=F32)
    o_ref[...] = res_ref[...] + mod_ref[2:3, :] * y


def _linear_res(stream, x, w, res, modt, tag):
    tm, tn = 1024, 1024
    k, n = w.shape
    blocks = (_nbytes((tm, k), BF16) + _nbytes((k, tn), BF16) + 2 * _nbytes((tm, tn), F32)
              + _nbytes((MOD_ROWS, tn), F32))
    return pl.pallas_call(
        _linear_res_kernel,
        grid=(stream.tokens // tm, n // tn),
        in_specs=[pl.BlockSpec((tm, k), lambda i, j: (i, 0)),
                  pl.BlockSpec((k, tn), lambda i, j: (0, j)),
                  pl.BlockSpec((tm, tn), lambda i, j: (i, j)),
                  _mod_spec(stream, tm, tn, col_block=True)],
        out_specs=pl.BlockSpec((tm, tn), lambda i, j: (i, j)),
        out_shape=jax.ShapeDtypeStruct((stream.tokens, n), F32),
        compiler_params=_params(blocks, 0, ("arbitrary", "arbitrary")),
        name="wo_%s_%s" % (tag, stream.name),
    )(x, w, res, modt)


def _pool_linear_res_kernel(x_ref, w_ref, b_ref, s_ref, res_ref, mod_ref, o_ref):
    y = (jnp.dot(x_ref[...], w_ref[...], preferred_element_type=F32) + b_ref[...]) * s_ref[...]
    o_ref[...] = res_ref[...] + mod_ref[2:3, :] * y


def _pool_linear_res(stream, mixed, w_pool, b_pool, pool_scale, res, modt):
    tm, pg = 1024, POOL_GROUP
    blocks = (_nbytes((tm, pg), BF16) + _nbytes((pg, pg), BF16) + 2 * _nbytes((1, pg), F32)
              + 2 * _nbytes((tm, pg), F32) + _nbytes((MOD_ROWS, pg), F32))
    return pl.pallas_call(
        _pool_linear_res_kernel,
        grid=(stream.tokens // tm, len(POOL_WINDOWS)),
        in_specs=[pl.BlockSpec((tm, pg), lambda i, j: (i, j)),
                  pl.BlockSpec((None, pg, pg), lambda i, j: (j, 0, 0)),
                  pl.BlockSpec((1, pg), lambda i, j: (0, j)),
                  pl.BlockSpec((1, pg), lambda i, j: (0, j)),
                  pl.BlockSpec((tm, pg), lambda i, j: (i, j)),
                  _mod_spec(stream, tm, pg, col_block=True)],
        out_specs=pl.BlockSpec((tm, pg), lambda i, j: (i, j)),
        out_shape=jax.ShapeDtypeStruct((stream.tokens, D), F32),
        compiler_params=_params(blocks, 0, ("arbitrary", "arbitrary")),
        name="pool_linear_" + stream.name,
    )(mixed, w_pool, b_pool.reshape(1, D), pool_scale.reshape(1, D), res, modt)


def _ln_linear_res_kernel(x_ref, lg_ref, lb_ref, w_ref, b_ref, res_ref, mod_ref, o_ref, xn_ref, pb_ref):
    @pl.when(pl.program_id(1) == 0)
    def _():
        _ln_silu_prologue(x_ref, lg_ref, lb_ref, pb_ref, xn_ref)

    y = jnp.dot(xn_ref[...], w_ref[...], preferred_element_type=F32) + b_ref[...]
    o_ref[...] = res_ref[...] + mod_ref[2:3, :] * y


def _ln_linear_res(stream, u, ln_g, ln_b, w, b, res, modt):
    tm, tn = 512, 1024
    blocks = (_nbytes((tm, D), F32) + 2 * _nbytes((1, D), F32) + _nbytes((D, tn), BF16) + _nbytes((1, tn), F32)
              + 2 * _nbytes((tm, tn), F32) + _nbytes((MOD_ROWS, tn), F32))
    resident = _nbytes((tm, D), BF16) + _nbytes((3, BF16_ROWS, D), F32)
    return pl.pallas_call(
        _ln_linear_res_kernel,
        grid=(stream.tokens // tm, D // tn),
        in_specs=[pl.BlockSpec((tm, D), lambda i, j: (i, 0)),
                  pl.BlockSpec((1, D), lambda i, j: (0, 0)),
                  pl.BlockSpec((1, D), lambda i, j: (0, 0)),
                  pl.BlockSpec((D, tn), lambda i, j: (0, j)),
                  pl.BlockSpec((1, tn), lambda i, j: (0, j)),
                  pl.BlockSpec((tm, tn), lambda i, j: (i, j)),
                  _mod_spec(stream, tm, tn, col_block=True)],
        out_specs=pl.BlockSpec((tm, tn), lambda i, j: (i, j)),
        out_shape=jax.ShapeDtypeStruct((stream.tokens, D), F32),
        scratch_shapes=[pltpu.VMEM((tm, D), BF16), pltpu.VMEM((3, BF16_ROWS, D), F32)],
        compiler_params=_params(blocks, resident, ("arbitrary", "arbitrary")),
        name="pw2_" + stream.name,
    )(u, ln_g.reshape(1, D), ln_b.reshape(1, D), w, b.reshape(1, D), res, modt)


MLP_TM = 512
MLP_TF = 512
MLP_STEPS = (CTX.tokens // MLP_TM) * (FFN // MLP_TF)


class CastJob(NamedTuple):
    src: jax.Array
    row_block: int
    col_block: int
    blocked: int


def _cast_specs(job, n_inner, n_steps):
    rows, cols = job.src.shape
    n_col = cols // job.col_block
    assert (rows // job.row_block) * n_col == n_steps, (job.src.shape, n_steps)

    def index(i, f):
        s = i * n_inner + f
        return s // n_col, s % n_col

    in_spec = pl.BlockSpec((job.row_block, job.col_block), index)
    if job.blocked:
        out_spec = pl.BlockSpec((job.col_block // job.blocked, job.row_block, job.blocked),
                                lambda i, f: (index(i, f)[1], index(i, f)[0], 0))
        out_shape = jax.ShapeDtypeStruct((cols // job.blocked, rows, job.blocked), BF16)
    else:
        out_spec = pl.BlockSpec((job.row_block, job.col_block), index)
        out_shape = jax.ShapeDtypeStruct((rows, cols), BF16)
    nbytes = _nbytes((job.row_block, job.col_block), F32) + _nbytes((job.row_block, job.col_block), BF16)
    return in_spec, out_spec, out_shape, nbytes


def _cast_block(src_ref, dst_ref):
    if len(dst_ref.shape) == 3:
        width = dst_ref.shape[2]
        for cb in range(dst_ref.shape[0]):
            dst_ref[cb] = src_ref[:, cb * width:(cb + 1) * width].astype(dst_ref.dtype)
    else:
        dst_ref[...] = src_ref[...].astype(dst_ref.dtype)


def _mlp_kernel(*refs, final_norm, n_casts):
    n_in = 5 + int(final_norm) + n_casts
    x_ref, g_ref, mod_ref, wup_ref, wdn_ref = refs[:5]
    gf_ref = refs[5] if final_norm else None
    cast_src = refs[n_in - n_casts:n_in]
    o_ref = refs[n_in]
    cast_dst = refs[n_in + 1:n_in + 1 + n_casts]
    xn_ref, pb_ref = refs[n_in + 1 + n_casts:]
    f = pl.program_id(1)

    for src_ref, dst_ref in zip(cast_src, cast_dst):
        _cast_block(src_ref, dst_ref)

    @pl.when(f == 0)
    def _():
        _rms_mod_prologue(x_ref, g_ref, mod_ref, 3, pb_ref, xn_ref)
        o_ref[...] = jnp.zeros_like(o_ref)

    up = jnp.dot(xn_ref[...], wup_ref[...], preferred_element_type=F32)
    act = jnp.square(jnp.maximum(up, 0.0)).astype(BF16)
    o_ref[...] += jnp.dot(act, wdn_ref[...], preferred_element_type=F32)

    @pl.when(f == pl.num_programs(1) - 1)
    def _():
        pb_ref[0] = _rows(mod_ref[5:6, :], BF16_ROWS)
        if final_norm:
            pb_ref[1] = _rows(gf_ref[...], BF16_ROWS)

        def body(rows):
            y = x_ref[rows, :] + pb_ref[0] * o_ref[rows, :]
            if final_norm:
                y = (y * lax.rsqrt(jnp.mean(y * y, axis=-1, keepdims=True) + EPS)) * pb_ref[1]
            o_ref[rows, :] = y

        _row_loop(x_ref.shape[0], body)


def _mlp(stream, x, g, modt, w_up, w_down, tag, g_final=None, casts=()):
    tm, tf = MLP_TM, MLP_TF
    n_tiles, n_inner = stream.tokens // tm, FFN // tf
    blocks = (2 * _nbytes((tm, D), F32) + _nbytes((1, D), F32) + _nbytes((MOD_ROWS, D), F32)
              + _nbytes((D, tf), BF16) + _nbytes((tf, D), BF16))
    resident = _nbytes((tm, D), BF16) + _nbytes((3, BF16_ROWS, D), F32)
    if w_up.ndim == 3:
        assert w_up.shape == (n_inner, D, tf)
        wup_spec = pl.BlockSpec((None, D, tf), lambda i, f: (f, 0, 0))
    else:
        wup_spec = pl.BlockSpec((D, tf), lambda i, f: (0, f))
    in_specs = [pl.BlockSpec((tm, D), lambda i, f: (i, 0)),
                pl.BlockSpec((1, D), lambda i, f: (0, 0)),
                _mod_spec(stream, tm),
                wup_spec,
                pl.BlockSpec((tf, D), lambda i, f: (f, 0))]
    args = [x, g.reshape(1, D), modt, w_up, w_down]
    if g_final is not None:
        in_specs.append(pl.BlockSpec((1, D), lambda i, f: (0, 0)))
        args.append(g_final.reshape(1, D))
        blocks += _nbytes((1, D), F32)
    out_specs = [pl.BlockSpec((tm, D), lambda i, f: (i, 0))]
    out_shape = [jax.ShapeDtypeStruct((stream.tokens, D), F32)]
    for job in casts:
        in_spec, out_spec, shape, nbytes = _cast_specs(job, n_inner, n_tiles * n_inner)
        in_specs.append(in_spec)
        args.append(job.src)
        out_specs.append(out_spec)
        out_shape.append(shape)
        blocks += nbytes
    return pl.pallas_call(
        functools.partial(_mlp_kernel, final_norm=g_final is not None, n_casts=len(casts)),
        grid=(n_tiles, n_inner),
        in_specs=in_specs,
        out_specs=out_specs,
        out_shape=out_shape,
        scratch_shapes=[pltpu.VMEM((tm, D), BF16), pltpu.VMEM((3, BF16_ROWS, D), F32)],
        compiler_params=_params(blocks, resident, ("arbitrary", "arbitrary")),
        name="mlp_%s_%s" % (tag, stream.name),
    )(*args)


def _dot_nt(a, b):
    return lax.dot_general(a, b, (((1,), (1,)), ((), ())), preferred_element_type=F32)


def _ctx_attn_kernel(q_ref, k_ref, v_ref, o_ref):
    scale = HEAD_DIM ** -0.5
    for hd in range(N_HEADS):
        cols = slice(hd * HEAD_DIM, (hd + 1) * HEAD_DIM)
        q = q_ref[:, cols]
        k = k_ref[:, cols].astype(BF16)
        v = v_ref[:, cols].astype(BF16)
        s = _dot_nt(q, k) * scale
        p = jnp.exp(s - jnp.max(s, axis=-1, keepdims=True))
        denom = jnp.sum(p, axis=-1, keepdims=True)
        o = jnp.dot(p.astype(BF16), v, preferred_element_type=F32) / denom
        o_ref[:, cols] = o.astype(o_ref.dtype)


def _ctx_attention(q, k, v):
    seq = CTX.seq_len
    blocks = 2 * _nbytes((seq, D), BF16) + 2 * _nbytes((seq, D), F32)
    return pl.pallas_call(
        _ctx_attn_kernel,
        grid=(CTX.tokens // seq,),
        in_specs=[pl.BlockSpec((seq, D), lambda b: (b, 0))] * 3,
        out_specs=pl.BlockSpec((seq, D), lambda b: (b, 0)),
        out_shape=jax.ShapeDtypeStruct((CTX.tokens, D), BF16),
        compiler_params=_params(blocks, 0, ("arbitrary",)),
        name="ctx_attention",
    )(q, k, v)


Q_CHUNK_ROWS = 4
CACHE_HEADS = 8


def _cache_heads_kernel(c_ref, o_ref):
    flat = c_ref.reshape(PAST * CACHE_HEADS, HEAD_DIM)
    for hd in range(CACHE_HEADS):
        o_ref[hd] = flat[pl.ds(hd, PAST, stride=CACHE_HEADS), :].astype(o_ref.dtype)


def _cache_heads(cache):
    nb = cache.shape[0]
    blocks = _nbytes((PAST, CACHE_HEADS, HEAD_DIM), F32) + _nbytes((CACHE_HEADS, PAST, HEAD_DIM), BF16)
    return pl.pallas_call(
        _cache_heads_kernel,
        grid=(nb, N_HEADS // CACHE_HEADS),
        in_specs=[pl.BlockSpec((None, PAST, CACHE_HEADS, HEAD_DIM), lambda b, g: (b, 0, g, 0))],
        out_specs=pl.BlockSpec((None, CACHE_HEADS, PAST, HEAD_DIM), lambda b, g: (b, g, 0, 0)),
        out_shape=jax.ShapeDtypeStruct((nb, N_HEADS, PAST, HEAD_DIM), BF16),
        compiler_params=_params(blocks, 0, ("arbitrary", "arbitrary")),
        name="cache_heads",
    )(cache)


def _row_start(r):
    return min(max(r - WIN_H // 2, 0), GRID_H - WIN_H)


def _chunk_key_rows(c):
    r0, r1 = c * Q_CHUNK_ROWS, (c + 1) * Q_CHUNK_ROWS - 1
    lo, hi = _row_start(r0), _row_start(r1) + WIN_H
    lo -= lo % 2
    hi += hi % 2
    return lo, hi


def _natten_kernel(q_ref, k_ref, v_ref, kc_ref, vc_ref, tab_ref, o_ref, bias_ref):
    scale = HEAD_DIM ** -0.5

    @pl.when(pl.program_id(1) == 0)
    def _():
        left = lax.broadcasted_iota(jnp.int32, (GRID_W, LANES), 1) < GRID_W
        masked = jnp.full((GRID_W, LANES), MASKED, F32)
        for r in range(GRID_H):
            rs = _row_start(r)
            for pair in range(GRID_H // 2):
                halves = []
                for kr in (2 * pair, 2 * pair + 1):
                    halves.append(tab_ref[kr - r + WIN_H - 1] if rs <= kr < rs + WIN_H else masked)
                bias_ref[r * GRID_W:(r + 1) * GRID_W, pair * LANES:(pair + 1) * LANES] = (
                    jnp.where(left, halves[0], halves[1]))

    kc = kc_ref[...]
    vc = vc_ref[...]
    nq = Q_CHUNK_ROWS * GRID_W
    for c in range(GRID_H // Q_CHUNK_ROWS):
        lo, hi = _chunk_key_rows(c)
        keys = slice(lo * GRID_W, hi * GRID_W)
        q = q_ref[c * nq:(c + 1) * nq, :]
        s_loc = _dot_nt(q, k_ref[keys, :]) * scale + bias_ref[c * nq:(c + 1) * nq, keys]
        s_ctx = _dot_nt(q, kc) * scale
        m = jnp.maximum(jnp.max(s_loc, axis=-1, keepdims=True), jnp.max(s_ctx, axis=-1, keepdims=True))
        p_loc = jnp.exp(s_loc - m)
        p_ctx = jnp.exp(s_ctx - m)
        denom = jnp.sum(p_loc, axis=-1, keepdims=True) + jnp.sum(p_ctx, axis=-1, keepdims=True)
        o = (jnp.dot(p_loc.astype(BF16), v_ref[keys, :], preferred_element_type=F32)
             + jnp.dot(p_ctx.astype(BF16), vc, preferred_element_type=F32)) / denom
        o_ref[c * nq:(c + 1) * nq, :] = o.astype(o_ref.dtype)


def _column_bias_tables(rpb):
    cols = np.arange(GRID_W)
    start = np.clip(cols - WIN_W // 2, 0, GRID_W - WIN_W)
    kc = cols[None, :]
    valid = (kc >= start[:, None]) & (kc < start[:, None] + WIN_W)
    dc = np.clip(kc - cols[:, None] + WIN_W - 1, 0, 2 * WIN_W - 2)
    tab = jnp.where(valid[None, None], rpb[:, :, dc], MASKED)
    return jnp.concatenate([tab, tab], axis=-1)


def _natten(qkv, cache_k, cache_v, rpb):
    tab = _column_bias_tables(rpb)
    n_dr = 2 * WIN_H - 1
    seq = LAT.seq_len
    blocks = (4 * _nbytes((seq, HEAD_DIM), BF16) + 2 * _nbytes((PAST, HEAD_DIM), BF16)
              + _nbytes((n_dr, GRID_W, LANES), F32))
    resident = _nbytes((seq, seq), F32)

    def qkv_spec(which):
        return pl.BlockSpec((None, None, seq, HEAD_DIM), lambda hd, b: (which, hd, b, 0))

    cache_spec = pl.BlockSpec((None, None, PAST, HEAD_DIM), lambda hd, b: (b, hd, 0, 0))
    return pl.pallas_call(
        _natten_kernel,
        grid=(N_HEADS, LAT.tokens // seq),
        in_specs=[qkv_spec(0), qkv_spec(1), qkv_spec(2), cache_spec, cache_spec,
                  pl.BlockSpec((None, n_dr, GRID_W, LANES), lambda hd, b: (hd, 0, 0, 0))],
        out_specs=pl.BlockSpec((seq, HEAD_DIM), lambda hd, b: (b, hd)),
        out_shape=jax.ShapeDtypeStruct((LAT.tokens, D), BF16),
        scratch_shapes=[pltpu.VMEM((seq, seq), F32)],
        compiler_params=_params(blocks, resident, ("arbitrary", "arbitrary")),
        name="natten",
    )(qkv, qkv, qkv, cache_k, cache_v, tab)


def _shift_rows(y, s, pos, seq_len):
    rolled = pltpu.roll(y, s % y.shape[0], axis=0)
    src = pos - s
    return jnp.where(jnp.logical_and(src >= 0, src < seq_len), rolled, 0.0)


def _window_sum(x, half, pos, seq_len):
    back = _shift_rows(x, 1, pos, seq_len)
    fwd = x
    n = 1
    while n < half:
        back = back + _shift_rows(back, n, pos, seq_len)
        fwd = fwd + _shift_rows(fwd, -n, pos, seq_len)
        n *= 2
    return back + fwd


def _pool_mix_kernel(x_ref, g_ref, mod_ref, o_ref, inv_ref, *, seq_len):
    def stats(rows):
        x = x_ref[rows, :]
        inv = lax.rsqrt(jnp.mean(x * x, axis=-1, keepdims=True) + EPS)
        inv_ref[rows, :] = jnp.broadcast_to(inv, (BF16_ROWS, LANES))

    _row_loop(GROUP, stats)

    for grp, window in enumerate(POOL_WINDOWS):
        half = window // 2

        def block(blk, carry, grp=grp, half=half):
            cols = pl.ds(pl.multiple_of(grp * POOL_GROUP + blk * LANES, LANES), LANES)
            pos = lax.broadcasted_iota(jnp.int32, (GROUP, LANES), 0) & (seq_len - 1)
            cnt = (jnp.minimum(pos + half, seq_len) - jnp.maximum(pos - half, 0)).astype(F32)
            up = ((x_ref[:, cols] * inv_ref[...]) * g_ref[:, cols]) * (1.0 + mod_ref[1:2, cols]) + mod_ref[0:1, cols]
            pooled = _window_sum(up, half, pos, seq_len) / cnt
            o_ref[:, cols] = (pooled - up).astype(o_ref.dtype)
            return carry

        lax.fori_loop(0, POOL_GROUP // LANES, block, 0)


def _pool_mix(stream, x, g, modt):
    blocks = _nbytes((GROUP, D), F32) + _nbytes((1, D), F32) + _nbytes((MOD_ROWS, D), F32) + _nbytes((GROUP, D), BF16)
    resident = _nbytes((GROUP, LANES), F32)
    return pl.pallas_call(
        functools.partial(_pool_mix_kernel, seq_len=stream.seq_len),
        grid=(stream.tokens // GROUP,),
        in_specs=[pl.BlockSpec((GROUP, D), lambda i: (i, 0)),
                  pl.BlockSpec((1, D), lambda i: (0, 0)),
                  _mod_spec(stream, GROUP)],
        out_specs=pl.BlockSpec((GROUP, D), lambda i: (i, 0)),
        out_shape=jax.ShapeDtypeStruct((stream.tokens, D), BF16),
        scratch_shapes=[pltpu.VMEM((GROUP, LANES), F32)],
        compiler_params=_params(blocks, resident, ("arbitrary",)),
        name="pool_mix_" + stream.name,
    )(x, g.reshape(1, D), modt)


CONV_HALO = 16
CONV_BLOCK = 512
CONV_ROWS = 64


def _dwconv_kernel(u_ref, w_ref, b_ref, o_ref, pad_ref, *, seq_len):
    n_seq = GROUP // seq_len
    stride = seq_len + CONV_HALO
    zeros = jnp.zeros((CONV_HALO, LANES), F32)
    for lb in range(CONV_BLOCK // LANES):
        cols = slice(lb * LANES, (lb + 1) * LANES)
        for s in range(n_seq + 1):
            pad_ref[lb, s * stride:s * stride + CONV_HALO, :] = zeros
        for s in range(n_seq):
            top = CONV_HALO + s * stride
            pad_ref[lb, top:top + seq_len, :] = u_ref[s * seq_len:(s + 1) * seq_len, cols]
        bias = jnp.broadcast_to(b_ref[:, cols], (SUBLANES, LANES))[None]
        for s in range(n_seq):
            for ch in range(seq_len // CONV_ROWS):
                first = CONV_HALO + s * stride + ch * CONV_ROWS - CONV_W // 2
                acc = jnp.broadcast_to(bias, (CONV_ROWS // SUBLANES, SUBLANES, LANES))
                for tap in range(CONV_W):
                    x = pad_ref[lb, first + tap:first + tap + CONV_ROWS, :]
                    acc = acc + x.reshape(CONV_ROWS // SUBLANES, SUBLANES, LANES) * w_ref[tap, :, cols][None]
                out_row = s * seq_len + ch * CONV_ROWS
                o_ref[out_row:out_row + CONV_ROWS, cols] = acc.reshape(CONV_ROWS, LANES)


def _dwconv(stream, u, w_dw, b_dw):
    w8 = jnp.broadcast_to(w_dw[:, None, :], (CONV_W, SUBLANES, D))
    n_seq = GROUP // stream.seq_len
    pad_rows = CONV_HALO + n_seq * (stream.seq_len + CONV_HALO)
    blocks = 2 * _nbytes((GROUP, CONV_BLOCK), F32) + _nbytes((CONV_W, SUBLANES, CONV_BLOCK), F32)
    resident = _nbytes((CONV_BLOCK // LANES, pad_rows, LANES), F32)
    return pl.pallas_call(
        functools.partial(_dwconv_kernel, seq_len=stream.seq_len),
        grid=(stream.tokens // GROUP, D // CONV_BLOCK),
        in_specs=[pl.BlockSpec((GROUP, CONV_BLOCK), lambda i, j: (i, j)),
                  pl.BlockSpec((CONV_W, SUBLANES, CONV_BLOCK), lambda i, j: (0, 0, j)),
                  pl.BlockSpec((1, CONV_BLOCK), lambda i, j: (0, j))],
        out_specs=pl.BlockSpec((GROUP, CONV_BLOCK), lambda i, j: (i, j)),
        out_shape=jax.ShapeDtypeStruct((stream.tokens, D), F32),
        scratch_shapes=[pltpu.VMEM((CONV_BLOCK // LANES, pad_rows, LANES), F32)],
        compiler_params=_params(blocks, resident, ("arbitrary", "arbitrary")),
        name="dwconv_" + stream.name,
    )(u, w8, b_dw.reshape(1, D))


def _attention_layer(hp, hs, modt, g_mix, w_qkv, w_o, rpb, cache_k, cache_v, tag):
    q_ctx, k_ctx, v_ctx = _qkv_ctx(hp, g_mix, modt, w_qkv)
    o_ctx = _ctx_attention(q_ctx, k_ctx, v_ctx)
    hp = _linear_res(CTX, o_ctx, w_o, hp, modt, tag)
    qkv_lat = _qkv_lat(hs, g_mix, modt, w_qkv)
    o_lat = _natten(qkv_lat, _cache_heads(cache_k), _cache_heads(cache_v), rpb)
    hs = _linear_res(LAT, o_lat, w_o, hs, modt, tag)
    new_shape = (CTX.tokens // CTX.seq_len, CTX.seq_len, N_HEADS, HEAD_DIM)
    return hp, hs, k_ctx.reshape(new_shape), v_ctx.reshape(new_shape)


def _cast_job(w, blocked=0):
    rows, cols = w.shape
    if rows // 32 == MLP_STEPS:
        return CastJob(w, 32, cols, blocked)
    assert (rows // BF16_ROWS) * 2 == MLP_STEPS, w.shape
    return CastJob(w, BF16_ROWS, cols // 2, blocked)


def _mlp_pair(hp, hs, g, modt, w_up, w_down, tag, g_final=None, next_up=None, next_down=None, extra=(None, None)):
    jobs_p = [_cast_job(w, MLP_TF if w is next_up else 0) for w in (next_up, extra[0]) if w is not None]
    jobs_s = [_cast_job(w) for w in (next_down, extra[1]) if w is not None]
    hp, *cast_p = _mlp(CTX, hp, g, modt, w_up, w_down, tag, g_final, jobs_p)
    hs, *cast_s = _mlp(LAT, hs, g, modt, w_up, w_down, tag, g_final, jobs_s)
    return hp, hs, cast_p, cast_s


def kernel(x_prompt, x_sample, cache_k_l0, cache_v_l0, cache_k_l3, cache_v_l3, c, c_ctx, l0_w_ada, l0_b_ada, l0_g_mix, l0_w_qkv, l0_w_o, l0_rpb, l0_g_mlp, l0_w_up, l0_w_down, l1_w_ada, l1_b_ada, l1_g_mix, l1_w_pool, l1_b_pool, l1_pool_scale, l1_g_mlp, l1_w_up, l1_w_down, l2_w_ada, l2_b_ada, l2_g_mix, l2_w_pw1, l2_b_pw1, l2_w_dw, l2_b_dw, l2_ln_g, l2_ln_b, l2_w_pw2, l2_b_pw2, l2_g_mlp, l2_w_up, l2_w_down, l3_w_ada, l3_b_ada, l3_g_mix, l3_w_qkv, l3_w_o, l3_rpb, l3_g_mlp, l3_w_up, l3_w_down, g_final):
    hp = x_prompt.reshape(CTX.tokens, D)
    hs = x_sample.reshape(LAT.tokens, D)
    cond = jnp.concatenate([c_ctx[None], c, jnp.zeros((N_COND - 1 - c.shape[0], D), F32)], axis=0)

    modt = _modulation_table(cond, l0_w_ada, l0_b_ada)
    hp, hs, new_k0, new_v0 = _attention_layer(hp, hs, modt, l0_g_mix, l0_w_qkv.astype(BF16), l0_w_o.astype(BF16),
                                              l0_rpb, cache_k_l0, cache_v_l0, "l0")
    hp, hs, (w_up1, w_pw1), (w_down1, w_pool) = _mlp_pair(
        hp, hs, l0_g_mlp, modt, l0_w_up.astype(BF16), l0_w_down.astype(BF16), "l0",
        next_up=l1_w_up, next_down=l1_w_down, extra=(l2_w_pw1, l1_w_pool.reshape(D, POOL_GROUP)))

    modt = _modulation_table(cond, l1_w_ada, l1_b_ada)
    w_pool = w_pool.reshape(l1_w_pool.shape)
    hp = _pool_linear_res(CTX, _pool_mix(CTX, hp, l1_g_mix, modt), w_pool, l1_b_pool, l1_pool_scale, hp, modt)
    hs = _pool_linear_res(LAT, _pool_mix(LAT, hs, l1_g_mix, modt), w_pool, l1_b_pool, l1_pool_scale, hs, modt)
    hp, hs, (w_up2, w_qkv3), (w_down2, w_pw2) = _mlp_pair(
        hp, hs, l1_g_mlp, modt, w_up1, w_down1, "l1",
        next_up=l2_w_up, next_down=l2_w_down, extra=(l3_w_qkv, l2_w_pw2))

    modt = _modulation_table(cond, l2_w_ada, l2_b_ada)
    up = _dwconv(CTX, _pw1_glu(CTX, hp, l2_g_mix, modt, w_pw1, l2_b_pw1), l2_w_dw, l2_b_dw)
    hp = _ln_linear_res(CTX, up, l2_ln_g, l2_ln_b, w_pw2, l2_b_pw2, hp, modt)
    us = _dwconv(LAT, _pw1_glu(LAT, hs, l2_g_mix, modt, w_pw1, l2_b_pw1), l2_w_dw, l2_b_dw)
    hs = _ln_linear_res(LAT, us, l2_ln_g, l2_ln_b, w_pw2, l2_b_pw2, hs, modt)
    hp, hs, (w_up3,), (w_down3, w_o3) = _mlp_pair(
        hp, hs, l2_g_mlp, modt, w_up2, w_down2, "l2",
        next_up=l3_w_up, next_down=l3_w_down, extra=(None, l3_w_o))

    modt = _modulation_table(cond, l3_w_ada, l3_b_ada)
    hp, hs, new_k3, new_v3 = _attention_layer(hp, hs, modt, l3_g_mix, w_qkv3, w_o3, l3_rpb,
                                              cache_k_l3, cache_v_l3, "l3")
    y_prompt, y_sample, _, _ = _mlp_pair(hp, hs, l3_g_mlp, modt, w_up3, w_down3, "l3", g_final)
    return (y_prompt.reshape(x_prompt.shape), y_sample.reshape(x_sample.shape), new_k0, new_v0, new_k3, new_v3)
```

```python
import functools
from typing import NamedTuple

import numpy as np
import jax
import jax.numpy as jnp
from jax import lax
from jax.experimental import pallas as pl
from jax.experimental.pallas import tpu as pltpu

D = 4096
N_HEADS = 32
HEAD_DIM = 128
FFN = 4 * D
GRID_W = 64
PAST = 512
WIN_H = 8
WIN_W = 16
CONV_W = 31
POOL_WINDOWS = (2, 4, 8, 16)
POOL_GROUP = D // len(POOL_WINDOWS)
N_MOD = 6
EPS = 1e-6
GROUP = 1024
MOD_ROWS = 8
N_COND = 16
MASKED = -1e30


class Stream(NamedTuple):
    name: str
    tokens: int
    seq_len: int
    mod_first: int
    mod_per_group: bool


CTX = Stream("ctx", 8192, 256, 0, False)
LAT = Stream("lat", 8192, 1024, 1, True)
GRID_H = LAT.seq_len // GRID_W

LANES = 128
SUBLANES = 8
BF16_ROWS = 16
ROW_LOOP_UNROLL = 4
VMEM_REQUEST_CAP_BYTES = 62 * 2**20
VMEM_COMPILER_MARGIN_BYTES = 12 * 2**20

F32 = jnp.float32
BF16 = jnp.bfloat16


def _nbytes(shape, dtype):
    return int(np.prod(shape)) * jnp.dtype(dtype).itemsize


def _params(double_buffered_bytes, resident_bytes, semantics):
    need = 2 * double_buffered_bytes + resident_bytes + VMEM_COMPILER_MARGIN_BYTES
    return pltpu.CompilerParams(
        dimension_semantics=semantics,
        vmem_limit_bytes=int(min(max(need, 16 * 2**20), VMEM_REQUEST_CAP_BYTES)))


def _mod_spec(stream, tm, cols=D, col_block=False):
    def index(i, j=0):
        row = stream.mod_first + ((i * tm) // GROUP if stream.mod_per_group else 0)
        return (row, 0, j if col_block else 0)
    return pl.BlockSpec((None, MOD_ROWS, cols), index)


def _rows(v, n):
    return jnp.broadcast_to(v, (n, v.shape[-1]))


def _ada_kernel(c_ref, w_ref, b_ref, o_ref):
    c = c_ref[...]
    s = (c * jax.nn.sigmoid(c)).astype(BF16)
    o_ref[...] = jnp.dot(s, w_ref[...].astype(BF16), preferred_element_type=F32) + b_ref[...]


def _modulation_table(cond, w_ada, b_ada):
    tn = 512
    n = N_MOD * D
    blocks = _nbytes((D, tn), F32) + _nbytes((1, tn), F32) + _nbytes((N_COND, tn), F32)
    resident = _nbytes((N_COND, D), F32) + _nbytes((D, tn), BF16)
    m = pl.pallas_call(
        _ada_kernel,
        grid=(n // tn,),
        in_specs=[pl.BlockSpec((N_COND, D), lambda j: (0, 0)),
                  pl.BlockSpec((D, tn), lambda j: (0, j)),
                  pl.BlockSpec((1, tn), lambda j: (0, j))],
        out_specs=pl.BlockSpec((N_COND, tn), lambda j: (0, j)),
        out_shape=jax.ShapeDtypeStruct((N_COND, n), F32),
        compiler_params=_params(blocks, resident, ("arbitrary",)),
        name="adaln",
    )(cond, w_ada, b_ada.reshape(1, n))
    m = m.reshape(N_COND, N_MOD, D)
    return jnp.concatenate([m, jnp.zeros((N_COND, MOD_ROWS - N_MOD, D), F32)], axis=1)


def _row_loop(n_rows, body):
    rc = BF16_ROWS

    def step(r, carry):
        body(pl.ds(pl.multiple_of(r * rc, rc), rc))
        return carry

    lax.fori_loop(0, n_rows // rc, step, 0, unroll=ROW_LOOP_UNROLL)


def _rms_mod_prologue(x_ref, g_ref, mod_ref, shift_row, pb_ref, xn_ref):
    pb_ref[0] = _rows(g_ref[...], BF16_ROWS)
    pb_ref[1] = _rows(1.0 + mod_ref[shift_row + 1:shift_row + 2, :], BF16_ROWS)
    pb_ref[2] = _rows(mod_ref[shift_row:shift_row + 1, :], BF16_ROWS)

    def body(rows):
        x = x_ref[rows, :]
        inv = lax.rsqrt(jnp.mean(x * x, axis=-1, keepdims=True) + EPS)
        xn_ref[rows, :] = (((x * inv) * pb_ref[0]) * pb_ref[1] + pb_ref[2]).astype(xn_ref.dtype)

    _row_loop(x_ref.shape[0], body)


def _ln_silu_prologue(x_ref, g_ref, b_ref, pb_ref, xn_ref):
    pb_ref[0] = _rows(g_ref[...], BF16_ROWS)
    pb_ref[1] = _rows(b_ref[...], BF16_ROWS)

    def body(rows):
        x = x_ref[rows, :]
        xc = x - jnp.mean(x, axis=-1, keepdims=True)
        inv = lax.rsqrt(jnp.mean(xc * xc, axis=-1, keepdims=True) + EPS)
        un = (xc * inv) * pb_ref[0] + pb_ref[1]
        xn_ref[rows, :] = (un * jax.nn.sigmoid(un)).astype(xn_ref.dtype)

    _row_loop(x_ref.shape[0], body)


def _qkv_ctx_kernel(x_ref, g_ref, mod_ref, w_ref, q_ref, k_ref, v_ref, xn_ref, pb_ref, *, n_each):
    j = pl.program_id(1)

    @pl.when(j == 0)
    def _():
        _rms_mod_prologue(x_ref, g_ref, mod_ref, 0, pb_ref, xn_ref)

    def proj():
        return jnp.dot(xn_ref[...], w_ref[...], preferred_element_type=F32)

    @pl.when(j < n_each)
    def _():
        q_ref[...] = proj().astype(q_ref.dtype)

    @pl.when(jnp.logical_and(j >= n_each, j < 2 * n_each))
    def _():
        k_ref[...] = proj()

    @pl.when(j >= 2 * n_each)
    def _():
        v_ref[...] = proj()


def _qkv_ctx(x, g, modt, w_qkv):
    tm, tn = 512, 1024
    n_each = D // tn
    t = CTX.tokens

    def park(lo):
        return lambda i, j: (i, jnp.clip(j - lo, 0, n_each - 1))

    blocks = (_nbytes((tm, D), F32) + _nbytes((1, D), F32) + _nbytes((MOD_ROWS, D), F32) + _nbytes((D, tn), BF16)
              + _nbytes((tm, tn), BF16) + 2 * _nbytes((tm, tn), F32))
    resident = _nbytes((tm, D), BF16) + _nbytes((3, BF16_ROWS, D), F32)
    return pl.pallas_call(
        functools.partial(_qkv_ctx_kernel, n_each=n_each),
        grid=(t // tm, 3 * n_each),
        in_specs=[pl.BlockSpec((tm, D), lambda i, j: (i, 0)),
                  pl.BlockSpec((1, D), lambda i, j: (0, 0)),
                  _mod_spec(CTX, tm),
                  pl.BlockSpec((D, tn), lambda i, j: (0, j))],
        out_specs=[pl.BlockSpec((tm, tn), park(0)),
                   pl.BlockSpec((tm, tn), park(n_each)),
                   pl.BlockSpec((tm, tn), park(2 * n_each))],
        out_shape=[jax.ShapeDtypeStruct((t, D), BF16),
                   jax.ShapeDtypeStruct((t, D), F32),
                   jax.ShapeDtypeStruct((t, D), F32)],
        scratch_shapes=[pltpu.VMEM((tm, D), BF16), pltpu.VMEM((3, BF16_ROWS, D), F32)],
        compiler_params=_params(blocks, resident, ("arbitrary", "arbitrary")),
        name="qkv_ctx",
    )(x, g.reshape(1, D), modt, w_qkv)


def _qkv_lat_kernel(x_ref, g_ref, mod_ref, w_ref, o_ref, xn_ref, pb_ref):
    @pl.when(pl.program_id(1) == 0)
    def _():
        _rms_mod_prologue(x_ref, g_ref, mod_ref, 0, pb_ref, xn_ref)

    y = jnp.dot(xn_ref[...], w_ref[...], preferred_element_type=F32)
    for hd in range(o_ref.shape[0]):
        o_ref[hd] = y[:, hd * HEAD_DIM:(hd + 1) * HEAD_DIM].astype(o_ref.dtype)


def _qkv_lat(x, g, modt, w_qkv):
    tm, tn = 512, 1024
    n_each = D // tn
    hpb = tn // HEAD_DIM
    t = LAT.tokens
    blocks = (_nbytes((tm, D), F32) + _nbytes((1, D), F32) + _nbytes((MOD_ROWS, D), F32) + _nbytes((D, tn), BF16)
              + _nbytes((tm, tn), BF16))
    resident = _nbytes((tm, D), BF16) + _nbytes((3, BF16_ROWS, D), F32)
    return pl.pallas_call(
        _qkv_lat_kernel,
        grid=(t // tm, 3 * n_each),
        in_specs=[pl.BlockSpec((tm, D), lambda i, j: (i, 0)),
                  pl.BlockSpec((1, D), lambda i, j: (0, 0)),
                  _mod_spec(LAT, tm),
                  pl.BlockSpec((D, tn), lambda i, j: (0, j))],
        out_specs=pl.BlockSpec((None, hpb, tm, HEAD_DIM), lambda i, j: (j // n_each, j % n_each, i, 0)),
        out_shape=jax.ShapeDtypeStruct((3, N_HEADS, t, HEAD_DIM), BF16),
        scratch_shapes=[pltpu.VMEM((tm, D), BF16), pltpu.VMEM((3, BF16_ROWS, D), F32)],
        compiler_params=_params(blocks, resident, ("arbitrary", "arbitrary")),
        name="qkv_lat",
    )(x, g.reshape(1, D), modt, w_qkv)


def _glu_kernel(x_ref, g_ref, mod_ref, wa_ref, wb_ref, ba_ref, bb_ref, o_ref, xn_ref, pb_ref):
    @pl.when(pl.program_id(1) == 0)
    def _():
        _rms_mod_prologue(x_ref, g_ref, mod_ref, 0, pb_ref, xn_ref)

    a = jnp.dot(xn_ref[...], wa_ref[...], preferred_element_type=F32) + ba_ref[...]
    b = jnp.dot(xn_ref[...], wb_ref[...], preferred_element_type=F32) + bb_ref[...]
    o_ref[...] = a * jax.nn.sigmoid(b)


def _pw1_glu(stream, x, g, modt, w_pw1, b_pw1):
    tm, tn = 512, 512
    nj = D // tn
    blocks = (_nbytes((tm, D), F32) + _nbytes((1, D), F32) + _nbytes((MOD_ROWS, D), F32) + 2 * _nbytes((D, tn), BF16)
              + 2 * _nbytes((1, tn), F32) + _nbytes((tm, tn), F32))
    resident = _nbytes((tm, D), BF16) + _nbytes((3, BF16_ROWS, D), F32)
    b2 = b_pw1.reshape(1, 2 * D)
    return pl.pallas_call(
        _glu_kernel,
        grid=(stream.tokens // tm, nj),
        in_specs=[pl.BlockSpec((tm, D), lambda i, j: (i, 0)),
                  pl.BlockSpec((1, D), lambda i, j: (0, 0)),
                  _mod_spec(stream, tm),
                  pl.BlockSpec((D, tn), lambda i, j: (0, j)),
                  pl.BlockSpec((D, tn), lambda i, j: (0, nj + j)),
                  pl.BlockSpec((1, tn), lambda i, j: (0, j)),
                  pl.BlockSpec((1, tn), lambda i, j: (0, nj + j))],
        out_specs=pl.BlockSpec((tm, tn), lambda i, j: (i, j)),
        out_shape=jax.ShapeDtypeStruct((stream.tokens, D), F32),
        scratch_shapes=[pltpu.VMEM((tm, D), BF16), pltpu.VMEM((3, BF16_ROWS, D), F32)],
        compiler_params=_params(blocks, resident, ("arbitrary", "arbitrary")),
        name="pw1_glu_" + stream.name,
    )(x, g.reshape(1, D), modt, w_pw1, w_pw1, b2, b2)


def _linear_res_kernel(x_ref, w_ref, res_ref, mod_ref, o_ref):
    y = jnp.dot(x_ref[...], w_ref[...], preferred_element_type=F32)
    o_ref[...] = res_ref[...] + mod_ref[2:3, :] * y


def _linear_res(stream, x, w, res, modt, tag):
    tm, tn = 1024, 1024
    k, n = w.shape
    blocks = (_nbytes((tm, k), BF16) + _nbytes((k, tn), BF16) + 2 * _nbytes((tm, tn), F32)
              + _nbytes((MOD_ROWS, tn), F32))
    return pl.pallas_call(
        _linear_res_kernel,
        grid=(stream.tokens // tm, n // tn),
        in_specs=[pl.BlockSpec((tm, k), lambda i, j: (i, 0)),
                  pl.BlockSpec((k, tn), lambda i, j: (0, j)),
                  pl.BlockSpec((tm, tn), lambda i, j: (i, j)),
                  _mod_spec(stream, tm, tn, col_block=True)],
        out_specs=pl.BlockSpec((tm, tn), lambda i, j: (i, j)),
        out_shape=jax.ShapeDtypeStruct((stream.tokens, n), F32),
        compiler_params=_params(blocks, 0, ("arbitrary", "arbitrary")),
        name="wo_%s_%s" % (tag, stream.name),
    )(x, w, res, modt)


def _pool_linear_res_kernel(x_ref, w_ref, b_ref, s_ref, res_ref, mod_ref, o_ref):
    y = (jnp.dot(x_ref[...], w_ref[...], preferred_element_type=F32) + b_ref[...]) * s_ref[...]
    o_ref[...] = res_ref[...] + mod_ref[2:3, :] * y


def _pool_linear_res(stream, mixed, w_pool, b_pool, pool_scale, res, modt):
    tm, pg = 1024, POOL_GROUP
    blocks = (_nbytes((tm, pg), BF16) + _nbytes((pg, pg), BF16) + 2 * _nbytes((1, pg), F32)
              + 2 * _nbytes((tm, pg), F32) + _nbytes((MOD_ROWS, pg), F32))
    return pl.pallas_call(
        _pool_linear_res_kernel,
        grid=(stream.tokens // tm, len(POOL_WINDOWS)),
        in_specs=[pl.BlockSpec((tm, pg), lambda i, j: (i, j)),
                  pl.BlockSpec((None, pg, pg), lambda i, j: (j, 0, 0)),
                  pl.BlockSpec((1, pg), lambda i, j: (0, j)),
                  pl.BlockSpec((1, pg), lambda i, j: (0, j)),
                  pl.BlockSpec((tm, pg), lambda i, j: (i, j)),
                  _mod_spec(stream, tm, pg, col_block=True)],
        out_specs=pl.BlockSpec((tm, pg), lambda i, j: (i, j)),
        out_shape=jax.ShapeDtypeStruct((stream.tokens, D), F32),
        compiler_params=_params(blocks, 0, ("arbitrary", "arbitrary")),
        name="pool_linear_" + stream.name,
    )(mixed, w_pool, b_pool.reshape(1, D), pool_scale.reshape(1, D), res, modt)


def _ln_linear_res_kernel(x_ref, lg_ref, lb_ref, w_ref, b_ref, res_ref, mod_ref, o_ref, xn_ref, pb_ref):
    @pl.when(pl.program_id(1) == 0)
    def _():
        _ln_silu_prologue(x_ref, lg_ref, lb_ref, pb_ref, xn_ref)

    y = jnp.dot(xn_ref[...], w_ref[...], preferred_element_type=F32) + b_ref[...]
    o_ref[...] = res_ref[...] + mod_ref[2:3, :] * y


def _ln_linear_res(stream, u, ln_g, ln_b, w, b, res, modt):
    tm, tn = 512, 1024
    blocks = (_nbytes((tm, D), F32) + 2 * _nbytes((1, D), F32) + _nbytes((D, tn), BF16) + _nbytes((1, tn), F32)
              + 2 * _nbytes((tm, tn), F32) + _nbytes((MOD_ROWS, tn), F32))
    resident = _nbytes((tm, D), BF16) + _nbytes((3, BF16_ROWS, D), F32)
    return pl.pallas_call(
        _ln_linear_res_kernel,
        grid=(stream.tokens // tm, D // tn),
        in_specs=[pl.BlockSpec((tm, D), lambda i, j: (i, 0)),
                  pl.BlockSpec((1, D), lambda i, j: (0, 0)),
                  pl.BlockSpec((1, D), lambda i, j: (0, 0)),
                  pl.BlockSpec((D, tn), lambda i, j: (0, j)),
                  pl.BlockSpec((1, tn), lambda i, j: (0, j)),
                  pl.BlockSpec((tm, tn), lambda i, j: (i, j)),
                  _mod_spec(stream, tm, tn, col_block=True)],
        out_specs=pl.BlockSpec((tm, tn), lambda i, j: (i, j)),
        out_shape=jax.ShapeDtypeStruct((stream.tokens, D), F32),
        scratch_shapes=[pltpu.VMEM((tm, D), BF16), pltpu.VMEM((3, BF16_ROWS, D), F32)],
        compiler_params=_params(blocks, resident, ("arbitrary", "arbitrary")),
        name="pw2_" + stream.name,
    )(u, ln_g.reshape(1, D), ln_b.reshape(1, D), w, b.reshape(1, D), res, modt)


MLP_TM = 512
MLP_TF = 512
MLP_STEPS = (CTX.tokens // MLP_TM) * (FFN // MLP_TF)


class CastJob(NamedTuple):
    src: jax.Array
    row_block: int
    col_block: int
    blocked: int


def _cast_specs(job, n_inner, n_steps):
    rows, cols = job.src.shape
    n_col = cols // job.col_block
    assert (rows // job.row_block) * n_col == n_steps, (job.src.shape, n_steps)

    def index(i, f):
        s = i * n_inner + f
        return s // n_col, s % n_col

    in_spec = pl.BlockSpec((job.row_block, job.col_block), index)
    if job.blocked:
        out_spec = pl.BlockSpec((job.col_block // job.blocked, job.row_block, job.blocked),
                                lambda i, f: (index(i, f)[1], index(i, f)[0], 0))
        out_shape = jax.ShapeDtypeStruct((cols // job.blocked, rows, job.blocked), BF16)
    else:
        out_spec = pl.BlockSpec((job.row_block, job.col_block), index)
        out_shape = jax.ShapeDtypeStruct((rows, cols), BF16)
    nbytes = _nbytes((job.row_block, job.col_block), F32) + _nbytes((job.row_block, job.col_block), BF16)
    return in_spec, out_spec, out_shape, nbytes


def _cast_block(src_ref, dst_ref):
    if len(dst_ref.shape) == 3:
        width = dst_ref.shape[2]
        for cb in range(dst_ref.shape[0]):
            dst_ref[cb] = src_ref[:, cb * width:(cb + 1) * width].astype(dst_ref.dtype)
    else:
        dst_ref[...] = src_ref[...].astype(dst_ref.dtype)


def _mlp_kernel(*refs, final_norm, n_casts):
    n_in = 5 + int(final_norm) + n_casts
    x_ref, g_ref, mod_ref, wup_ref, wdn_ref = refs[:5]
    gf_ref = refs[5] if final_norm else None
    cast_src = refs[n_in - n_casts:n_in]
    o_ref = refs[n_in]
    cast_dst = refs[n_in + 1:n_in + 1 + n_casts]
    xn_ref, pb_ref = refs[n_in + 1 + n_casts:]
    f = pl.program_id(1)

    @pl.when(f == 0)
    def _():
        _rms_mod_prologue(x_ref, g_ref, mod_ref, 3, pb_ref, xn_ref)
        o_ref[...] = jnp.zeros_like(o_ref)

    for src_ref, dst_ref in zip(cast_src, cast_dst):
        _cast_block(src_ref, dst_ref)

    up = jnp.dot(xn_ref[...], wup_ref[...], preferred_element_type=F32)
    act = jnp.square(jnp.maximum(up, 0.0)).astype(BF16)
    o_ref[...] += jnp.dot(act, wdn_ref[...], preferred_element_type=F32)

    @pl.when(f == pl.num_programs(1) - 1)
    def _():
        pb_ref[0] = _rows(mod_ref[5:6, :], BF16_ROWS)
        if final_norm:
            pb_ref[1] = _rows(gf_ref[...], BF16_ROWS)

        def body(rows):
            y = x_ref[rows, :] + pb_ref[0] * o_ref[rows, :]
            if final_norm:
                y = (y * lax.rsqrt(jnp.mean(y * y, axis=-1, keepdims=True) + EPS)) * pb_ref[1]
            o_ref[rows, :] = y

        _row_loop(x_ref.shape[0], body)


def _mlp(stream, x, g, modt, w_up, w_down, tag, g_final=None, casts=()):
    tm, tf = MLP_TM, MLP_TF
    n_tiles, n_inner = stream.tokens // tm, FFN // tf
    blocks = (2 * _nbytes((tm, D), F32) + _nbytes((1, D), F32) + _nbytes((MOD_ROWS, D), F32)
              + _nbytes((D, tf), BF16) + _nbytes((tf, D), BF16))
    resident = _nbytes((tm, D), BF16) + _nbytes((3, BF16_ROWS, D), F32)
    if w_up.ndim == 3:
        assert w_up.shape == (n_inner, D, tf)
        wup_spec = pl.BlockSpec((None, D, tf), lambda i, f: (f, 0, 0))
    else:
        wup_spec = pl.BlockSpec((D, tf), lambda i, f: (0, f))
    in_specs = [pl.BlockSpec((tm, D), lambda i, f: (i, 0)),
                pl.BlockSpec((1, D), lambda i, f: (0, 0)),
                _mod_spec(stream, tm),
                wup_spec,
                pl.BlockSpec((tf, D), lambda i, f: (f, 0))]
    args = [x, g.reshape(1, D), modt, w_up, w_down]
    if g_final is not None:
        in_specs.append(pl.BlockSpec((1, D), lambda i, f: (0, 0)))
        args.append(g_final.reshape(1, D))
        blocks += _nbytes((1, D), F32)
    out_specs = [pl.BlockSpec((tm, D), lambda i, f: (i, 0))]
    out_shape = [jax.ShapeDtypeStruct((stream.tokens, D), F32)]
    for job in casts:
        in_spec, out_spec, shape, nbytes = _cast_specs(job, n_inner, n_tiles * n_inner)
        in_specs.append(in_spec)
        args.append(job.src)
        out_specs.append(out_spec)
        out_shape.append(shape)
        blocks += nbytes
    return pl.pallas_call(
        functools.partial(_mlp_kernel, final_norm=g_final is not None, n_casts=len(casts)),
        grid=(n_tiles, n_inner),
        in_specs=in_specs,
        out_specs=out_specs,
        out_shape=out_shape,
        scratch_shapes=[pltpu.VMEM((tm, D), BF16), pltpu.VMEM((3, BF16_ROWS, D), F32)],
        compiler_params=_params(blocks, resident, ("arbitrary", "arbitrary")),
        name="mlp_%s_%s" % (tag, stream.name),
    )(*args)


def _dot_nt(a, b):
    return lax.dot_general(a, b, (((1,), (1,)), ((), ())), preferred_element_type=F32)


def _ctx_attn_kernel(q_ref, k_ref, v_ref, o_ref):
    scale = HEAD_DIM ** -0.5
    for hd in range(N_HEADS):
        cols = slice(hd * HEAD_DIM, (hd + 1) * HEAD_DIM)
        q = q_ref[:, cols]
        k = k_ref[:, cols].astype(BF16)
        v = v_ref[:, cols].astype(BF16)
        s = _dot_nt(q, k) * scale
        p = jnp.exp(s - jnp.max(s, axis=-1, keepdims=True))
        denom = jnp.sum(p, axis=-1, keepdims=True)
        o = jnp.dot(p.astype(BF16), v, preferred_element_type=F32) / denom
        o_ref[:, cols] = o.astype(o_ref.dtype)


def _ctx_attention(q, k, v):
    seq = CTX.seq_len
    blocks = 2 * _nbytes((seq, D), BF16) + 2 * _nbytes((seq, D), F32)
    return pl.pallas_call(
        _ctx_attn_kernel,
        grid=(CTX.tokens // seq,),
        in_specs=[pl.BlockSpec((seq, D), lambda b: (b, 0))] * 3,
        out_specs=pl.BlockSpec((seq, D), lambda b: (b, 0)),
        out_shape=jax.ShapeDtypeStruct((CTX.tokens, D), BF16),
        compiler_params=_params(blocks, 0, ("arbitrary",)),
        name="ctx_attention",
    )(q, k, v)


Q_CHUNK_ROWS = 4
CACHE_HEADS = 8


def _cache_heads_kernel(c_ref, o_ref):
    flat = c_ref.reshape(PAST * CACHE_HEADS, HEAD_DIM)
    for hd in range(CACHE_HEADS):
        o_ref[hd] = flat[pl.ds(hd, PAST, stride=CACHE_HEADS), :].astype(o_ref.dtype)


def _cache_heads(cache):
    nb = cache.shape[0]
    blocks = _nbytes((PAST, CACHE_HEADS, HEAD_DIM), F32) + _nbytes((CACHE_HEADS, PAST, HEAD_DIM), BF16)
    return pl.pallas_call(
        _cache_heads_kernel,
        grid=(nb, N_HEADS // CACHE_HEADS),
        in_specs=[pl.BlockSpec((None, PAST, CACHE_HEADS, HEAD_DIM), lambda b, g: (b, 0, g, 0))],
        out_specs=pl.BlockSpec((None, CACHE_HEADS, PAST, HEAD_DIM), lambda b, g: (b, g, 0, 0)),
        out_shape=jax.ShapeDtypeStruct((nb, N_HEADS, PAST, HEAD_DIM), BF16),
        compiler_params=_params(blocks, 0, ("arbitrary", "arbitrary")),
        name="cache_heads",
    )(cache)


def _row_start(r):
    return min(max(r - WIN_H // 2, 0), GRID_H - WIN_H)


def _chunk_key_rows(c):
    r0, r1 = c * Q_CHUNK_ROWS, (c + 1) * Q_CHUNK_ROWS - 1
    lo, hi = _row_start(r0), _row_start(r1) + WIN_H
    lo -= lo % 2
    hi += hi % 2
    return lo, hi


def _natten_kernel(*refs, n_casts):
    q_ref, k_ref, v_ref, kc_ref, vc_ref, tab_ref = refs[:6]
    cast_src = refs[6:6 + n_casts]
    o_ref = refs[6 + n_casts]
    cast_dst = refs[7 + n_casts:7 + 2 * n_casts]
    bias_ref = refs[7 + 2 * n_casts]
    scale = HEAD_DIM ** -0.5

    @pl.when(pl.program_id(1) == 0)
    def _():
        left = lax.broadcasted_iota(jnp.int32, (GRID_W, LANES), 1) < GRID_W
        masked = jnp.full((GRID_W, LANES), MASKED, F32)
        for r in range(GRID_H):
            rs = _row_start(r)
            for pair in range(GRID_H // 2):
                halves = []
                for kr in (2 * pair, 2 * pair + 1):
                    halves.append(tab_ref[kr - r + WIN_H - 1] if rs <= kr < rs + WIN_H else masked)
                bias_ref[r * GRID_W:(r + 1) * GRID_W, pair * LANES:(pair + 1) * LANES] = (
                    jnp.where(left, halves[0], halves[1]))

    for src_ref, dst_ref in zip(cast_src, cast_dst):
        _cast_block(src_ref, dst_ref)

    kc = kc_ref[...]
    vc = vc_ref[...]
    nq = Q_CHUNK_ROWS * GRID_W
    for c in range(GRID_H // Q_CHUNK_ROWS):
        lo, hi = _chunk_key_rows(c)
        keys = slice(lo * GRID_W, hi * GRID_W)
        q = q_ref[c * nq:(c + 1) * nq, :]
        s_loc = _dot_nt(q, k_ref[keys, :]) * scale + bias_ref[c * nq:(c + 1) * nq, keys]
        s_ctx = _dot_nt(q, kc) * scale
        m = jnp.maximum(jnp.max(s_loc, axis=-1, keepdims=True), jnp.max(s_ctx, axis=-1, keepdims=True))
        p_loc = jnp.exp(s_loc - m)
        p_ctx = jnp.exp(s_ctx - m)
        denom = jnp.sum(p_loc, axis=-1, keepdims=True) + jnp.sum(p_ctx, axis=-1, keepdims=True)
        o = (jnp.dot(p_loc.astype(BF16), v_ref[keys, :], preferred_element_type=F32)
             + jnp.dot(p_ctx.astype(BF16), vc, preferred_element_type=F32)) / denom
        o_ref[c * nq:(c + 1) * nq, :] = o.astype(o_ref.dtype)


def _column_bias_tables(rpb):
    cols = np.arange(GRID_W)
    start = np.clip(cols - WIN_W // 2, 0, GRID_W - WIN_W)
    kc = cols[None, :]
    valid = (kc >= start[:, None]) & (kc < start[:, None] + WIN_W)
    dc = np.clip(kc - cols[:, None] + WIN_W - 1, 0, 2 * WIN_W - 2)
    tab = jnp.where(valid[None, None], rpb[:, :, dc], MASKED)
    return jnp.concatenate([tab, tab], axis=-1)


NATTEN_STEPS = N_HEADS * (LAT.tokens // LAT.seq_len)


def _natten(qkv, cache_k, cache_v, rpb, cast_weights=()):
    tab = _column_bias_tables(rpb)
    n_dr = 2 * WIN_H - 1
    seq = LAT.seq_len
    n_seq = LAT.tokens // seq
    blocks = (4 * _nbytes((seq, HEAD_DIM), BF16) + 2 * _nbytes((PAST, HEAD_DIM), BF16)
              + _nbytes((n_dr, GRID_W, LANES), F32))
    resident = _nbytes((seq, seq), F32)

    def qkv_spec(which):
        return pl.BlockSpec((None, None, seq, HEAD_DIM), lambda hd, b: (which, hd, b, 0))

    cache_spec = pl.BlockSpec((None, None, PAST, HEAD_DIM), lambda hd, b: (b, hd, 0, 0))
    in_specs = [qkv_spec(0), qkv_spec(1), qkv_spec(2), cache_spec, cache_spec,
                pl.BlockSpec((None, n_dr, GRID_W, LANES), lambda hd, b: (hd, 0, 0, 0))]
    args = [qkv, qkv, qkv, cache_k, cache_v, tab]
    out_specs = [pl.BlockSpec((seq, HEAD_DIM), lambda hd, b: (b, hd))]
    out_shape = [jax.ShapeDtypeStruct((LAT.tokens, D), BF16)]
    for w in cast_weights:
        in_spec, out_spec, shape, nbytes = _cast_specs(_cast_job(w, NATTEN_STEPS), n_seq, NATTEN_STEPS)
        in_specs.append(in_spec)
        args.append(w)
        out_specs.append(out_spec)
        out_shape.append(shape)
        blocks += nbytes
    return pl.pallas_call(
        functools.partial(_natten_kernel, n_casts=len(cast_weights)),
        grid=(N_HEADS, n_seq),
        in_specs=in_specs,
        out_specs=out_specs,
        out_shape=out_shape,
        scratch_shapes=[pltpu.VMEM((seq, seq), F32)],
        compiler_params=_params(blocks, resident, ("arbitrary", "arbitrary")),
        name="natten",
    )(*args)


def _shift_rows(y, s, pos, seq_len):
    rolled = pltpu.roll(y, s % y.shape[0], axis=0)
    src = pos - s
    return jnp.where(jnp.logical_and(src >= 0, src < seq_len), rolled, 0.0)


def _window_sum(x, half, pos, seq_len):
    back = _shift_rows(x, 1, pos, seq_len)
    fwd = x
    n = 1
    while n < half:
        back = back + _shift_rows(back, n, pos, seq_len)
        fwd = fwd + _shift_rows(fwd, -n, pos, seq_len)
        n *= 2
    return back + fwd


def _pool_mix_kernel(x_ref, g_ref, mod_ref, o_ref, inv_ref, *, seq_len):
    def stats(rows):
        x = x_ref[rows, :]
        inv = lax.rsqrt(jnp.mean(x * x, axis=-1, keepdims=True) + EPS)
        inv_ref[rows, :] = jnp.broadcast_to(inv, (BF16_ROWS, LANES))

    _row_loop(GROUP, stats)

    for grp, window in enumerate(POOL_WINDOWS):
        half = window // 2

        def block(blk, carry, grp=grp, half=half):
            cols = pl.ds(pl.multiple_of(grp * POOL_GROUP + blk * LANES, LANES), LANES)
            pos = lax.broadcasted_iota(jnp.int32, (GROUP, LANES), 0) & (seq_len - 1)
            cnt = (jnp.minimum(pos + half, seq_len) - jnp.maximum(pos - half, 0)).astype(F32)
            up = ((x_ref[:, cols] * inv_ref[...]) * g_ref[:, cols]) * (1.0 + mod_ref[1:2, cols]) + mod_ref[0:1, cols]
            pooled = _window_sum(up, half, pos, seq_len) / cnt
            o_ref[:, cols] = (pooled - up).astype(o_ref.dtype)
            return carry

        lax.fori_loop(0, POOL_GROUP // LANES, block, 0)


def _pool_mix(stream, x, g, modt):
    blocks = _nbytes((GROUP, D), F32) + _nbytes((1, D), F32) + _nbytes((MOD_ROWS, D), F32) + _nbytes((GROUP, D), BF16)
    resident = _nbytes((GROUP, LANES), F32)
    return pl.pallas_call(
        functools.partial(_pool_mix_kernel, seq_len=stream.seq_len),
        grid=(stream.tokens // GROUP,),
        in_specs=[pl.BlockSpec((GROUP, D), lambda i: (i, 0)),
                  pl.BlockSpec((1, D), lambda i: (0, 0)),
                  _mod_spec(stream, GROUP)],
        out_specs=pl.BlockSpec((GROUP, D), lambda i: (i, 0)),
        out_shape=jax.ShapeDtypeStruct((stream.tokens, D), BF16),
        scratch_shapes=[pltpu.VMEM((GROUP, LANES), F32)],
        compiler_params=_params(blocks, resident, ("arbitrary",)),
        name="pool_mix_" + stream.name,
    )(x, g.reshape(1, D), modt)


CONV_HALO = 16
CONV_BLOCK = 512
CONV_ROWS = 64


def _dwconv_kernel(u_ref, w_ref, b_ref, o_ref, pad_ref, *, seq_len):
    n_seq = GROUP // seq_len
    stride = seq_len + CONV_HALO
    zeros = jnp.zeros((CONV_HALO, LANES), F32)
    for lb in range(CONV_BLOCK // LANES):
        cols = slice(lb * LANES, (lb + 1) * LANES)
        for s in range(n_seq + 1):
            pad_ref[lb, s * stride:s * stride + CONV_HALO, :] = zeros
        for s in range(n_seq):
            top = CONV_HALO + s * stride
            pad_ref[lb, top:top + seq_len, :] = u_ref[s * seq_len:(s + 1) * seq_len, cols]
        bias = jnp.broadcast_to(b_ref[:, cols], (SUBLANES, LANES))[None]
        for s in range(n_seq):
            for ch in range(seq_len // CONV_ROWS):
                first = CONV_HALO + s * stride + ch * CONV_ROWS - CONV_W // 2
                acc = jnp.broadcast_to(bias, (CONV_ROWS // SUBLANES, SUBLANES, LANES))
                for tap in range(CONV_W):
                    x = pad_ref[lb, first + tap:first + tap + CONV_ROWS, :]
                    acc = acc + x.reshape(CONV_ROWS // SUBLANES, SUBLANES, LANES) * w_ref[tap, :, cols][None]
                out_row = s * seq_len + ch * CONV_ROWS
                o_ref[out_row:out_row + CONV_ROWS, cols] = acc.reshape(CONV_ROWS, LANES)


def _dwconv(stream, u, w_dw, b_dw):
    w8 = jnp.broadcast_to(w_dw[:, None, :], (CONV_W, SUBLANES, D))
    n_seq = GROUP // stream.seq_len
    pad_rows = CONV_HALO + n_seq * (stream.seq_len + CONV_HALO)
    blocks = 2 * _nbytes((GROUP, CONV_BLOCK), F32) + _nbytes((CONV_W, SUBLANES, CONV_BLOCK), F32)
    resident = _nbytes((CONV_BLOCK // LANES, pad_rows, LANES), F32)
    return pl.pallas_call(
        functools.partial(_dwconv_kernel, seq_len=stream.seq_len),
        grid=(stream.tokens // GROUP, D // CONV_BLOCK),
        in_specs=[pl.BlockSpec((GROUP, CONV_BLOCK), lambda i, j: (i, j)),
                  pl.BlockSpec((CONV_W, SUBLANES, CONV_BLOCK), lambda i, j: (0, 0, j)),
                  pl.BlockSpec((1, CONV_BLOCK), lambda i, j: (0, j))],
        out_specs=pl.BlockSpec((GROUP, CONV_BLOCK), lambda i, j: (i, j)),
        out_shape=jax.ShapeDtypeStruct((stream.tokens, D), F32),
        scratch_shapes=[pltpu.VMEM((CONV_BLOCK // LANES, pad_rows, LANES), F32)],
        compiler_params=_params(blocks, resident, ("arbitrary", "arbitrary")),
        name="dwconv_" + stream.name,
    )(u, w8, b_dw.reshape(1, D))


def _attention_layer(hp, hs, modt, g_mix, w_qkv, w_o, rpb, cache_k, cache_v, tag, cast_weights=()):
    q_ctx, k_ctx, v_ctx = _qkv_ctx(hp, g_mix, modt, w_qkv)
    o_ctx = _ctx_attention(q_ctx, k_ctx, v_ctx)
    hp = _linear_res(CTX, o_ctx, w_o, hp, modt, tag)
    qkv_lat = _qkv_lat(hs, g_mix, modt, w_qkv)
    o_lat, *cast = _natten(qkv_lat, _cache_heads(cache_k), _cache_heads(cache_v), rpb, cast_weights)
    hs = _linear_res(LAT, o_lat, w_o, hs, modt, tag)
    new_shape = (CTX.tokens // CTX.seq_len, CTX.seq_len, N_HEADS, HEAD_DIM)
    return hp, hs, k_ctx.reshape(new_shape), v_ctx.reshape(new_shape), cast


def _cast_job(w, n_steps, blocked=0):
    rows, cols = w.shape
    if rows % n_steps == 0 and (rows // n_steps) % BF16_ROWS == 0:
        return CastJob(w, rows // n_steps, cols, blocked)
    assert (rows // BF16_ROWS) * 2 == n_steps, (w.shape, n_steps)
    return CastJob(w, BF16_ROWS, cols // 2, blocked)


def _mlp_pair(hp, hs, g, modt, w_up, w_down, tag, g_final=None, next_up=None, next_down=None, extra=(None, None)):
    jobs_p = [_cast_job(w, MLP_STEPS, MLP_TF if w is next_up else 0) for w in (next_up, extra[0]) if w is not None]
    jobs_s = [_cast_job(w, MLP_STEPS) for w in (next_down, extra[1]) if w is not None]
    hp, *cast_p = _mlp(CTX, hp, g, modt, w_up, w_down, tag, g_final, jobs_p)
    hs, *cast_s = _mlp(LAT, hs, g, modt, w_up, w_down, tag, g_final, jobs_s)
    return hp, hs, cast_p, cast_s


def kernel(x_prompt, x_sample, cache_k_l0, cache_v_l0, cache_k_l3, cache_v_l3, c, c_ctx, l0_w_ada, l0_b_ada, l0_g_mix, l0_w_qkv, l0_w_o, l0_rpb, l0_g_mlp, l0_w_up, l0_w_down, l1_w_ada, l1_b_ada, l1_g_mix, l1_w_pool, l1_b_pool, l1_pool_scale, l1_g_mlp, l1_w_up, l1_w_down, l2_w_ada, l2_b_ada, l2_g_mix, l2_w_pw1, l2_b_pw1, l2_w_dw, l2_b_dw, l2_ln_g, l2_ln_b, l2_w_pw2, l2_b_pw2, l2_g_mlp, l2_w_up, l2_w_down, l3_w_ada, l3_b_ada, l3_g_mix, l3_w_qkv, l3_w_o, l3_rpb, l3_g_mlp, l3_w_up, l3_w_down, g_final):
    hp = x_prompt.reshape(CTX.tokens, D)
    hs = x_sample.reshape(LAT.tokens, D)
    cond = jnp.concatenate([c_ctx[None], c, jnp.zeros((N_COND - 1 - c.shape[0], D), F32)], axis=0)

    modt = _modulation_table(cond, l0_w_ada, l0_b_ada)
    hp, hs, new_k0, new_v0, (w_up0, w_down0) = _attention_layer(
        hp, hs, modt, l0_g_mix, l0_w_qkv.astype(BF16), l0_w_o.astype(BF16), l0_rpb, cache_k_l0, cache_v_l0, "l0",
        cast_weights=(l0_w_up, l0_w_down))
    hp, hs, (w_up1, w_pw1), (w_down1, w_pool) = _mlp_pair(
        hp, hs, l0_g_mlp, modt, w_up0, w_down0, "l0",
        next_up=l1_w_up, next_down=l1_w_down, extra=(l2_w_pw1, l1_w_pool.reshape(D, POOL_GROUP)))

    modt = _modulation_table(cond, l1_w_ada, l1_b_ada)
    w_pool = w_pool.reshape(l1_w_pool.shape)
    hp = _pool_linear_res(CTX, _pool_mix(CTX, hp, l1_g_mix, modt), w_pool, l1_b_pool, l1_pool_scale, hp, modt)
    hs = _pool_linear_res(LAT, _pool_mix(LAT, hs, l1_g_mix, modt), w_pool, l1_b_pool, l1_pool_scale, hs, modt)
    hp, hs, (w_up2, w_qkv3), (w_down2, w_pw2) = _mlp_pair(
        hp, hs, l1_g_mlp, modt, w_up1, w_down1, "l1",
        next_up=l2_w_up, next_down=l2_w_down, extra=(l3_w_qkv, l2_w_pw2))

    modt = _modulation_table(cond, l2_w_ada, l2_b_ada)
    up = _dwconv(CTX, _pw1_glu(CTX, hp, l2_g_mix, modt, w_pw1, l2_b_pw1), l2_w_dw, l2_b_dw)
    hp = _ln_linear_res(CTX, up, l2_ln_g, l2_ln_b, w_pw2, l2_b_pw2, hp, modt)
    us = _dwconv(LAT, _pw1_glu(LAT, hs, l2_g_mix, modt, w_pw1, l2_b_pw1), l2_w_dw, l2_b_dw)
    hs = _ln_linear_res(LAT, us, l2_ln_g, l2_ln_b, w_pw2, l2_b_pw2, hs, modt)
    hp, hs, (w_up3,), (w_down3, w_o3) = _mlp_pair(
        hp, hs, l2_g_mlp, modt, w_up2, w_down2, "l2",
        next_up=l3_w_up, next_down=l3_w_down, extra=(None, l3_w_o))

    modt = _modulation_table(cond, l3_w_ada, l3_b_ada)
    hp, hs, new_k3, new_v3, _ = _attention_layer(hp, hs, modt, l3_g_mix, w_qkv3, w_o3, l3_rpb,
                                                 cache_k_l3, cache_v_l3, "l3")
    y_prompt, y_sample, _, _ = _mlp_pair(hp, hs, l3_g_mlp, modt, w_up3, w_down3, "l3", g_final)
    return (y_prompt.reshape(x_prompt.shape), y_sample.reshape(x_sample.shape), new_k0, new_v0, new_k3, new_v3)
```
